```python
import jax, jax.numpy as jnp
from jax import lax
import numpy as np

D_MODEL = 1024
BATCH = 8
SEQ = 8192
DEPTH = 2

CHUNK = 64
N_MEM = 256
RWKV_HEADS = 8
RWKV_HEAD_DIM = 64
D_RWKV = RWKV_HEADS * RWKV_HEAD_DIM
DECAY_LORA = 64
ICLR_LORA = 64
GATE_LORA = 160
RWKV_COLS = 3 * D_RWKV + DECAY_LORA + ICLR_LORA + GATE_LORA
D_CONV = 256
CONV_WIDTH = 31
XATTN_HEADS = 4
XATTN_HEAD_DIM = 64
D_XATTN = XATTN_HEADS * XATTN_HEAD_DIM
N_BRANCH = 3
N_IN = RWKV_COLS + 2 * D_CONV + D_XATTN + N_BRANCH * D_MODEL
D_FF = 2816
FFN_CONV_WIDTH = 3

RMS_EPS = 1e-6
LN_EPS = 1e-5
GN_EPS = 64e-5

kernel_name = 'rwkv7_conformer_xattn_gated_hybrid'


def rms_norm(x, g):
    xf = x.astype(jnp.float32)
    y = xf * lax.rsqrt(jnp.mean(xf * xf, axis=-1, keepdims=True) + RMS_EPS)
    return (y * g.astype(jnp.float32)).astype(x.dtype)


def layer_norm(x, g, b):
    xf = x.astype(jnp.float32)
    mu = jnp.mean(xf, axis=-1, keepdims=True)
    var = jnp.mean(jnp.square(xf - mu), axis=-1, keepdims=True)
    y = (xf - mu) * lax.rsqrt(var + LN_EPS)
    return (y * g.astype(jnp.float32) + b.astype(jnp.float32)).astype(x.dtype)


def token_shift(p):
    return jnp.pad(p, ((0, 0), (1, 0), (0, 0)))[:, :-1]


def causal_dwconv(x, w, b):
    K, C = w.shape
    y = lax.conv_general_dilated(
        x, w[:, None, :].astype(x.dtype), window_strides=(1,), padding=[(K - 1, 0)],
        dimension_numbers=('NWC', 'WIO', 'NWC'), feature_group_count=C)
    return y + b.astype(x.dtype)


def rwkv7_time_mix(p, mu, w0, w_up, a0, a_up, g_up, k_k, k_a, r_k, gn_w, gn_b, w_o):
    B, S, _ = p.shape
    H, N = RWKV_HEADS, RWKV_HEAD_DIM
    f32 = jnp.float32
    p = p + mu * (token_shift(p) - p)
    r, k, v, wd, ad, gd = jnp.split(
        p, [D_RWKV, 2 * D_RWKV, 3 * D_RWKV, 3 * D_RWKV + DECAY_LORA,
            3 * D_RWKV + DECAY_LORA + ICLR_LORA], axis=-1)
    w_log = -jax.nn.softplus(-(w0 + jnp.tanh(wd) @ w_up)) - 0.5
    decay = jnp.exp(-jnp.exp(w_log.astype(f32)))
    a = jax.nn.sigmoid(a0 + ad @ a_up)
    g = jax.nn.sigmoid(gd) @ g_up
    kk = (k * k_k).astype(f32).reshape(B, S, H, N)
    kk = kk / jnp.maximum(jnp.sqrt(jnp.sum(kk * kk, axis=-1, keepdims=True)), 1e-12)
    k = k * (1.0 + (a - 1.0) * k_a)
    heads = lambda t: t.astype(f32).reshape(B, S, H, N)
    rh, kh, vh, ah, wh = heads(r), heads(k), heads(v), heads(a), heads(decay)
    tmaj = lambda t: jnp.moveaxis(t, 1, 0)

    def step(state, inp):
        r_t, w_t, k_t, v_t, kk_t, a_t = inp
        sa = jnp.einsum('bhvk,bhk->bhv', state, -kk_t)
        state = (state * w_t[:, :, None, :]
                 + sa[..., None] * (kk_t * a_t)[:, :, None, :]
                 + v_t[..., None] * k_t[:, :, None, :])
        return state, jnp.einsum('bhvk,bhk->bhv', state, r_t)

    s0 = jnp.zeros((B, H, N, N), f32)
    _, y = lax.scan(step, s0, (tmaj(rh), tmaj(wh), tmaj(kh), tmaj(vh), tmaj(kk), tmaj(ah)))
    y = jnp.moveaxis(y, 0, 1)
    m = jnp.mean(y, axis=-1, keepdims=True)
    var = jnp.mean(jnp.square(y - m), axis=-1, keepdims=True)
    y = ((y - m) * lax.rsqrt(var + GN_EPS)).reshape(B, S, D_RWKV)
    y = y * gn_w.astype(f32) + gn_b.astype(f32)
    bonus = jnp.sum(rh * kh * r_k.astype(f32), axis=-1, keepdims=True) * vh
    y = (y + bonus.reshape(B, S, D_RWKV)) * g.astype(f32)
    return y.astype(p.dtype) @ w_o


def conformer_conv(p, w_dw, b_dw, ln_g, ln_b, w_o, b_o):
    val, gate = jnp.split(p, 2, axis=-1)
    u = val * jax.nn.sigmoid(gate)
    u = causal_dwconv(u, w_dw, b_dw)
    u = jax.nn.silu(layer_norm(u, ln_g, ln_b))
    return u @ w_o + b_o


def memory_cross_attention(q, mem, mem_g, w_kv, w_o):
    B, S, _ = q.shape
    M = mem.shape[1]
    k, v = jnp.split(rms_norm(mem, mem_g) @ w_kv, 2, axis=-1)
    qh = q.reshape(B, S, XATTN_HEADS, XATTN_HEAD_DIM)
    kh = k.reshape(B, M, XATTN_HEADS, XATTN_HEAD_DIM)
    vh = v.reshape(B, M, XATTN_HEADS, XATTN_HEAD_DIM)
    s = jnp.einsum('bshd,bmhd->bhsm', qh, kh).astype(jnp.float32) * (XATTN_HEAD_DIM ** -0.5)
    pr = jax.nn.softmax(s, axis=-1).astype(q.dtype)
    o = jnp.einsum('bhsm,bmhd->bshd', pr, vh).reshape(B, S, D_XATTN)
    return o @ w_o


def conv_ffn(h, w_up, w_dw, b_dw, w_down):
    z = causal_dwconv(h @ w_up, w_dw, b_dw)
    u, g = jnp.split(z, 2, axis=-1)
    return (jax.nn.silu(g) * u) @ w_down


def setup_inputs(seed: int = 0) -> dict:
    key = jax.random.key(seed)
    ks = iter(jax.random.split(key, 40))
    f32 = jnp.float32
    L = DEPTH
    res_scale = (2 * DEPTH) ** -0.5

    def nrm(shape, scale):
        return jax.random.normal(next(ks), shape, f32) * scale

    def unif(shape, lo, hi):
        return jax.random.uniform(next(ks), shape, f32, lo, hi)

    return {
        'x': nrm((BATCH, SEQ, D_MODEL), 1.0),
        'mem': nrm((BATCH, N_MEM, D_MODEL), 1.0),
        'attn_norm_g': 1.0 + nrm((L, D_MODEL), 0.02),
        'w_in': nrm((L, D_MODEL, N_IN), D_MODEL ** -0.5),
        'b_gate': nrm((L, N_BRANCH * D_MODEL), 0.1),
        'rwkv_mu': unif((L, RWKV_COLS), 0.0, 1.0),
        'rwkv_w0': unif((L, D_RWKV), -4.0, 1.0),
        'rwkv_w_up': nrm((L, DECAY_LORA, D_RWKV), 0.5 * DECAY_LORA ** -0.5),
        'rwkv_a0': nrm((L, D_RWKV), 0.1),
        'rwkv_a_up': nrm((L, ICLR_LORA, D_RWKV), 0.5 * ICLR_LORA ** -0.5),
        'rwkv_g_up': nrm((L, GATE_LORA, D_RWKV), GATE_LORA ** -0.5),
        'rwkv_k_k': 0.85 + nrm((L, D_RWKV), 0.02),
        'rwkv_k_a': 1.0 + nrm((L, D_RWKV), 0.02),
        'rwkv_r_k': nrm((L, RWKV_HEADS, RWKV_HEAD_DIM), 0.1),
        'rwkv_gn_w': 1.0 + nrm((L, D_RWKV), 0.02),
        'rwkv_gn_b': nrm((L, D_RWKV), 0.02),
        'rwkv_w_o': nrm((L, D_RWKV, D_MODEL), D_RWKV ** -0.5),
        'conv_w_dw': nrm((L, CONV_WIDTH, D_CONV), CONV_WIDTH ** -0.5),
        'conv_b_dw': nrm((L, D_CONV), 0.02),
        'conv_ln_g': 1.0 + nrm((L, D_CONV), 0.02),
        'conv_ln_b': nrm((L, D_CONV), 0.02),
        'conv_w_o': nrm((L, D_CONV, D_MODEL), D_CONV ** -0.5),
        'conv_b_o': nrm((L, D_MODEL), 0.02),
        'mem_norm_g': 1.0 + nrm((L, D_MODEL), 0.02),
        'xattn_w_kv': nrm((L, D_MODEL, 2 * D_XATTN), D_MODEL ** -0.5),
        'xattn_w_o': nrm((L, D_XATTN, D_MODEL), D_XATTN ** -0.5),
        'w_out': nrm((L, D_MODEL, D_MODEL), D_MODEL ** -0.5 * res_scale),
        'ffn_norm_g': 1.0 + nrm((L, D_MODEL), 0.02),
        'ffn_w_up': nrm((L, D_MODEL, 2 * D_FF), D_MODEL ** -0.5),
        'ffn_w_dw': nrm((L, FFN_CONV_WIDTH, 2 * D_FF), FFN_CONV_WIDTH ** -0.5),
        'ffn_b_dw': nrm((L, 2 * D_FF), 0.02),
        'ffn_w_down': nrm((L, D_FF, D_MODEL), D_FF ** -0.5 * res_scale),
        'final_norm_g': 1.0 + nrm((D_MODEL,), 0.02),
    }


def reference(x, mem, attn_norm_g, w_in, b_gate, rwkv_mu, rwkv_w0, rwkv_w_up, rwkv_a0,
              rwkv_a_up, rwkv_g_up, rwkv_k_k, rwkv_k_a, rwkv_r_k, rwkv_gn_w, rwkv_gn_b,
              rwkv_w_o, conv_w_dw, conv_b_dw, conv_ln_g, conv_ln_b, conv_w_o, conv_b_o,
              mem_norm_g, xattn_w_kv, xattn_w_o, w_out, ffn_norm_g, ffn_w_up, ffn_w_dw,
              ffn_b_dw, ffn_w_down, final_norm_g):
    c1 = RWKV_COLS
    c2 = c1 + 2 * D_CONV
    c3 = c2 + D_XATTN
    for l in range(DEPTH):
        h = rms_norm(x, attn_norm_g[l])
        p = h @ w_in[l]
        p_rwkv, p_conv, p_q, p_gate = jnp.split(p, [c1, c2, c3], axis=-1)
        y_a = rwkv7_time_mix(p_rwkv, rwkv_mu[l], rwkv_w0[l], rwkv_w_up[l], rwkv_a0[l],
                             rwkv_a_up[l], rwkv_g_up[l], rwkv_k_k[l], rwkv_k_a[l],
                             rwkv_r_k[l], rwkv_gn_w[l], rwkv_gn_b[l], rwkv_w_o[l])
        y_b = conformer_conv(p_conv, conv_w_dw[l], conv_b_dw[l], conv_ln_g[l], conv_ln_b[l],
                             conv_w_o[l], conv_b_o[l])
        y_c = memory_cross_attention(p_q, mem, mem_norm_g[l], xattn_w_kv[l], xattn_w_o[l])
        g_a, g_b, g_c = jnp.split(jax.nn.sigmoid(p_gate + b_gate[l]), N_BRANCH, axis=-1)
        x = x + (g_a * y_a + g_b * y_b + g_c * y_c) @ w_out[l]
        x = x + conv_ffn(rms_norm(x, ffn_norm_g[l]), ffn_w_up[l], ffn_w_dw[l], ffn_b_dw[l],
                         ffn_w_down[l])
    return rms_norm(x, final_norm_g)
```

```python
import functools

import jax
import jax.numpy as jnp
from jax import lax
from jax.experimental import pallas as pl
from jax.experimental.pallas import tpu as pltpu

F32 = jnp.float32
BF16 = jnp.bfloat16

D_MODEL = 1024
N_MEM = 256
RWKV_HEADS = 8
HEAD_DIM = 64
D_RWKV = RWKV_HEADS * HEAD_DIM
DECAY_LORA = 64
ICLR_LORA = 64
GATE_LORA = 160
D_CONV = 256
CONV_WIDTH = 31
XATTN_HEADS = 4
D_XATTN = XATTN_HEADS * HEAD_DIM
N_BRANCH = 3
D_FF = 2816
FFN_CONV_WIDTH = 3
RMS_EPS = 1e-6
LN_EPS = 1e-5
GN_EPS = 64e-5

LANES = 128
SUBLANES = 8
CHUNK = 64
PAIR = 2 * HEAD_DIM
N_PAIR = D_RWKV // PAIR

WD_PAD = LANES
AD_PAD = LANES
GD_PAD = 2 * LANES
C_WD = 3 * D_RWKV
C_AD = C_WD + WD_PAD
C_GD = C_AD + AD_PAD
C_CONV = C_GD + GD_PAD
C_Q = C_CONV + 2 * D_CONV
C_GATE = C_Q + D_XATTN
N_IN_PAD = C_GATE + N_BRANCH * D_MODEL

HIST = 32
FF_BLOCK = 256
SEQ_TILE = 256
VMEM_LIMIT_BYTES = 56 * 1024 * 1024


def _dot(a, b):
    return jnp.dot(a, b, preferred_element_type=F32)


def _dot_nt(a, b):
    return lax.dot_general(a, b, (((1,), (1,)), ((), ())), preferred_element_type=F32)


def _sigmoid(x):
    return 1.0 / (1.0 + jnp.exp(-x))


def _split3(x):
    hi = x.astype(BF16)
    r1 = x - hi.astype(F32)
    mid = r1.astype(BF16)
    lo = (r1 - mid.astype(F32)).astype(BF16)
    return hi, mid, lo


def _dot_left_exact(m_bf16, x):
    hi, mid, lo = _split3(x)
    return _dot(m_bf16, hi) + _dot(m_bf16, mid) + _dot(m_bf16, lo)


def _dot_right_exact(x, m_bf16):
    hi, mid, lo = _split3(x)
    return _dot(hi, m_bf16) + _dot(mid, m_bf16) + _dot(lo, m_bf16)


def _two(x):
    lane = lax.broadcasted_iota(jnp.int32, x.shape, 1)
    first = lane < HEAD_DIM
    return jnp.concatenate([jnp.where(first, x, 0.0), jnp.where(first, 0.0, x)], axis=0)


def _rwkv_pair_chunk(ah, rh, bh, kh, v, bk_end_t, pl_col, state):
    t_idx = lax.broadcasted_iota(jnp.int32, (CHUNK, PAIR), 0)
    j_idx = lax.broadcasted_iota(jnp.int32, (CHUNK, PAIR), 1) & (HEAD_DIM - 1)
    strict = j_idx < t_idx
    incl = j_idx <= t_idx

    lhs = jnp.concatenate([ah, rh], axis=0)
    rhs_t = jnp.concatenate([_two(bh), _two(kh)], axis=0)
    sc = _dot_nt(lhs, rhs_t)
    m_ab = jnp.where(strict, sc[0:CHUNK, 0:PAIR], 0.0)
    m_ak = jnp.where(strict, sc[0:CHUNK, PAIR:2 * PAIR], 0.0)
    n_rb = jnp.where(incl, sc[CHUNK:2 * CHUNK, 0:PAIR], 0.0)
    n_rk = jnp.where(incl, sc[CHUNK:2 * CHUNK, PAIR:2 * PAIR], 0.0)

    t_inv = jnp.where(j_idx == t_idx, 1.0, 0.0) + m_ab
    m_pow = _dot(m_ab, _two(m_ab))
    for _ in range(4):
        res = _dot(m_pow, jnp.concatenate([_two(m_pow), _two(t_inv)], axis=1))
        m_pow = res[:, 0:PAIR]
        t_inv = t_inv + res[:, PAIR:2 * PAIR]
    t_inv = t_inv + _dot(m_pow, _two(t_inv))

    makv = _dot(m_ak, _two(v))
    res = _dot(t_inv, jnp.concatenate([_two(ah), _two(makv)], axis=1))
    w_til = res[:, 0:PAIR]
    u_til = res[:, PAIR:2 * PAIR]

    res = _dot(jnp.concatenate([w_til, rh], axis=0), state)
    u = res[0:CHUNK] + u_til
    y = res[CHUNK:2 * CHUNK] + _dot(
        jnp.concatenate([n_rb, n_rk], axis=1),
        jnp.concatenate([_two(u), _two(v)], axis=0))
    upd = _dot(bk_end_t, jnp.concatenate([u, v], axis=0))
    r_idx = lax.broadcasted_iota(jnp.int32, (PAIR, PAIR), 0)
    c_idx = lax.broadcasted_iota(jnp.int32, (PAIR, PAIR), 1)
    same_head = (r_idx < HEAD_DIM) == (c_idx < HEAD_DIM)
    new_state = state * pl_col + jnp.where(same_head, upd, 0.0)
    return y, new_state


def _row_to_col(row):
    r_idx = lax.broadcasted_iota(jnp.int32, (PAIR, PAIR), 0)
    c_idx = lax.broadcasted_iota(jnp.int32, (PAIR, PAIR), 1)
    return jnp.sum(jnp.where(r_idx == c_idx, jnp.broadcast_to(row, (PAIR, PAIR)), 0.0),
                   axis=1, keepdims=True)


def _mixer_kernel(x_ref, kt_ref, vb_ref, g_ref, win_ref, bg_ref, mu_ref, w0_ref, wup_ref,
                  a0_ref, aup_ref, gup_ref, kkw_ref, ka_ref, rk_ref, gnw_ref, gnb_ref, wo_ref,
                  cw_ref, cb_ref, lng_ref, lnb_ref, cwo_ref, cbo_ref, xwo_ref, wout_ref,
                  ones_ref, tri_ref, o_ref,
                  prev_ref, hist_ref, state_ref, y_ref, ubuf_ref):
    ts = x_ref.shape[0]
    n_chunk = ts // CHUNK

    @pl.when(pl.program_id(1) == 0)
    def _():
        prev_ref[...] = jnp.zeros_like(prev_ref)
        hist_ref[...] = jnp.zeros_like(hist_ref)
        state_ref[...] = jnp.zeros_like(state_ref)

    x = x_ref[...]
    ms = jnp.mean(x * x, axis=-1, keepdims=True)
    hb = ((x * lax.rsqrt(ms + RMS_EPS)) * g_ref[...]).astype(BF16)
    row = lax.broadcasted_iota(jnp.int32, (ts, 1), 0)

    p = _dot(hb, win_ref[:, 0:C_CONV])
    shifted = jnp.where(row == 0, prev_ref[0:1, :], pltpu.roll(p, 1, 0))
    prev_ref[0:1, :] = p[ts - 1:ts, :]
    p = p + mu_ref[...] * (shifted - p)
    r = p[:, 0:D_RWKV]
    k = p[:, D_RWKV:2 * D_RWKV]
    v = p[:, 2 * D_RWKV:3 * D_RWKV]
    wd = p[:, C_WD:C_AD]
    ad = p[:, C_AD:C_GD]
    gd = p[:, C_GD:C_CONV]

    z = w0_ref[...] + _dot(jnp.tanh(wd).astype(BF16), wup_ref[...])
    lw = (-jnp.exp(F32(-0.5))) * _sigmoid(z)
    a = _sigmoid(a0_ref[...] + _dot(ad.astype(BF16), aup_ref[...]))
    g_out = _dot(_sigmoid(gd).astype(BF16), gup_ref[...])

    ones_bd = ones_ref[...]
    kk = k * kkw_ref[...]
    ssq = _dot_right_exact(kk * kk, ones_bd)
    kk = kk / jnp.maximum(jnp.sqrt(ssq), 1e-12)
    k = k * (1.0 + (a - 1.0) * ka_ref[...])

    cum = _dot_left_exact(tri_ref[...], lw)
    cum_end = jnp.concatenate(
        [jnp.broadcast_to(cum[(c + 1) * CHUNK - 1:(c + 1) * CHUNK, :], (CHUNK, D_RWKV))
         for c in range(n_chunk)], axis=0)
    e_inv = jnp.exp(-cum)
    e_end = jnp.exp(cum_end - cum)
    kka = kk * a
    rh_all = r * jnp.exp(cum)
    ah_all = -kk * jnp.exp(cum - lw)
    bh_all = kka * e_inv
    kh_all = k * e_inv
    b_end_all = kka * e_end
    k_end_all = k * e_end
    pl_all = jnp.exp(cum_end)

    for q in range(N_PAIR):
        cs = slice(q * PAIR, (q + 1) * PAIR)
        state = state_ref[q]
        for c in range(n_chunk):
            rs = slice(c * CHUNK, (c + 1) * CHUNK)
            bk_end = jnp.concatenate([b_end_all[rs, cs], k_end_all[rs, cs]], axis=0)
            pl_col = _row_to_col(pl_all[c * CHUNK:c * CHUNK + 1, cs])
            y_c, state = _rwkv_pair_chunk(ah_all[rs, cs], rh_all[rs, cs], bh_all[rs, cs],
                                          kh_all[rs, cs], v[rs, cs], bk_end.T, pl_col, state)
            y_ref[rs, cs] = y_c
        state_ref[q] = state

    y = y_ref[...]
    inv_n = F32(1.0 / HEAD_DIM)
    mean = _dot_right_exact(y, ones_bd) * inv_n
    dev = y - mean
    var = _dot_right_exact(dev * dev, ones_bd) * inv_n
    y = dev * lax.rsqrt(var + GN_EPS) * gnw_ref[...] + gnb_ref[...]
    bonus = _dot_right_exact(r * k * rk_ref[...], ones_bd) * v
    y_a = _dot(((y + bonus) * g_out).astype(BF16), wo_ref[...])

    gate = _sigmoid(_dot(hb, win_ref[:, C_GATE:C_GATE + D_MODEL]) + bg_ref[:, 0:D_MODEL])
    merged = gate * y_a

    pc = _dot(hb, win_ref[:, C_CONV:C_Q])
    u = pc[:, 0:D_CONV] * _sigmoid(pc[:, D_CONV:2 * D_CONV])
    ubuf_ref[0:HIST, :] = hist_ref[...]
    ubuf_ref[HIST:HIST + ts, :] = u
    hist_ref[...] = u[ts - HIST:ts, :]
    acc = jnp.broadcast_to(cb_ref[...], (ts, D_CONV))
    first = HIST - (CONV_WIDTH - 1)
    for tap in range(CONV_WIDTH):
        acc = acc + cw_ref[tap:tap + 1, :] * ubuf_ref[first + tap:first + tap + ts, :]
    mu_ln = jnp.mean(acc, axis=-1, keepdims=True)
    dev = acc - mu_ln
    var = jnp.mean(dev * dev, axis=-1, keepdims=True)
    u = dev * lax.rsqrt(var + LN_EPS) * lng_ref[...] + lnb_ref[...]
    u = u * _sigmoid(u)
    y_b = _dot(u.astype(BF16), cwo_ref[...]) + cbo_ref[...]

    gate = _sigmoid(_dot(hb, win_ref[:, C_GATE + D_MODEL:C_GATE + 2 * D_MODEL])
                    + bg_ref[:, D_MODEL:2 * D_MODEL])
    merged = merged + gate * y_b

    qv = _dot(hb, win_ref[:, C_Q:C_GATE])
    sc = _dot(qv.astype(BF16), kt_ref[...])
    probs = []
    for h in range(XATTN_HEADS):
        s_h = sc[:, h * N_MEM:(h + 1) * N_MEM]
        e_h = jnp.exp(s_h - jnp.max(s_h, axis=-1, keepdims=True))
        probs.append(e_h / jnp.sum(e_h, axis=-1, keepdims=True))
    pr = jnp.concatenate(probs, axis=1).astype(BF16)
    o_c = _dot(pr, vb_ref[...])
    y_c = _dot(o_c.astype(BF16), xwo_ref[...])

    gate = _sigmoid(_dot(hb, win_ref[:, C_GATE + 2 * D_MODEL:C_GATE + 3 * D_MODEL])
                    + bg_ref[:, 2 * D_MODEL:3 * D_MODEL])
    merged = merged + gate * y_c

    o_ref[...] = x + _dot(merged.astype(BF16), wout_ref[...])


def _ffn_kernel(x_ref, g_ref, wup_ref, wdw_ref, bdw_ref, wdn_ref, fg_ref, o_ref, carry_ref,
                *, final_norm):
    ts = x_ref.shape[0]

    @pl.when(pl.program_id(1) == 0)
    def _():
        carry_ref[...] = jnp.zeros_like(carry_ref)

    x = x_ref[...]
    ms = jnp.mean(x * x, axis=-1, keepdims=True)
    hb = ((x * lax.rsqrt(ms + RMS_EPS)) * g_ref[...]).astype(BF16)
    row = lax.broadcasted_iota(jnp.int32, (ts, 1), 0)

    def conv_block(col):
        cs = slice(col, col + FF_BLOCK)
        zb = _dot(hb, wup_ref[:, cs])
        tail = carry_ref[:, cs]
        carry_ref[:, cs] = zb[ts - SUBLANES:ts, :]
        z1 = jnp.where(row == 0, tail[SUBLANES - 1:SUBLANES, :], pltpu.roll(zb, 1, 0))
        z2 = jnp.where(row == 0, tail[SUBLANES - 2:SUBLANES - 1, :],
                       jnp.where(row == 1, tail[SUBLANES - 1:SUBLANES, :], pltpu.roll(zb, 2, 0)))
        return (wdw_ref[0:1, cs] * z2 + wdw_ref[1:2, cs] * z1 + wdw_ref[2:3, cs] * zb
                + bdw_ref[:, cs])

    acc = jnp.zeros((ts, D_MODEL), F32)
    for j in range(D_FF // FF_BLOCK):
        u = conv_block(j * FF_BLOCK)
        gt = conv_block(D_FF + j * FF_BLOCK)
        act = (gt * _sigmoid(gt) * u).astype(BF16)
        acc = acc + _dot(act, wdn_ref[j * FF_BLOCK:(j + 1) * FF_BLOCK, :])
    out = x + acc
    if final_norm:
        ms = jnp.mean(out * out, axis=-1, keepdims=True)
        out = out * lax.rsqrt(ms + RMS_EPS) * fg_ref[...]
    o_ref[...] = out


def _kv_kernel(mem_ref, g_ref, wkv_ref, kt_ref, vb_ref):
    m = mem_ref[...]
    ms = jnp.mean(m * m, axis=-1, keepdims=True)
    mb = ((m * lax.rsqrt(ms + RMS_EPS)) * g_ref[...]).astype(BF16)
    kv = _dot(mb, wkv_ref[...])
    k_t = kv[:, 0:D_XATTN].T * F32(HEAD_DIM ** -0.5)
    vv = kv[:, D_XATTN:2 * D_XATTN]
    r_idx = lax.broadcasted_iota(jnp.int32, (D_XATTN, XATTN_HEADS * N_MEM), 0)
    c_idx = lax.broadcasted_iota(jnp.int32, (D_XATTN, XATTN_HEADS * N_MEM), 1)
    kt_ref[...] = jnp.where(r_idx // HEAD_DIM == c_idx // N_MEM,
                            jnp.concatenate([k_t] * XATTN_HEADS, axis=1), 0.0).astype(BF16)
    r_idx = lax.broadcasted_iota(jnp.int32, (XATTN_HEADS * N_MEM, D_XATTN), 0)
    c_idx = lax.broadcasted_iota(jnp.int32, (XATTN_HEADS * N_MEM, D_XATTN), 1)
    vb_ref[...] = jnp.where(r_idx // N_MEM == c_idx // HEAD_DIM,
                            jnp.concatenate([vv] * XATTN_HEADS, axis=0), 0.0).astype(BF16)


def _const_spec(shape, layer):
    nd = len(shape)
    return pl.BlockSpec((None,) + tuple(shape[1:]), lambda b, s: (layer,) + (0,) * (nd - 1),
                        pipeline_mode=pl.Buffered(1))


def _pad_axis(x, axis, new):
    pad = [(0, 0)] * x.ndim
    pad[axis] = (0, new - x.shape[axis])
    return jnp.pad(x, pad)


def _row(x):
    return x.reshape(x.shape[0], 1, -1)


def kernel(x, mem, attn_norm_g, w_in, b_gate, rwkv_mu, rwkv_w0, rwkv_w_up, rwkv_a0,
           rwkv_a_up, rwkv_g_up, rwkv_k_k, rwkv_k_a, rwkv_r_k, rwkv_gn_w, rwkv_gn_b,
           rwkv_w_o, conv_w_dw, conv_b_dw, conv_ln_g, conv_ln_b, conv_w_o, conv_b_o,
           mem_norm_g, xattn_w_kv, xattn_w_o, w_out, ffn_norm_g, ffn_w_up, ffn_w_dw,
           ffn_b_dw, ffn_w_down, final_norm_g):
    batch, seq, _ = x.shape
    depth = w_in.shape[0]
    ts = min(SEQ_TILE, seq)
    assert seq % ts == 0 and ts % CHUNK == 0 and ts >= HIST
    n_seq = seq // ts
    c1 = 3 * D_RWKV
    c_rw = c1 + DECAY_LORA + ICLR_LORA + GATE_LORA

    def pad_cols(w):
        return jnp.concatenate([
            w[..., 0:c1],
            _pad_axis(w[..., c1:c1 + DECAY_LORA], -1, WD_PAD),
            _pad_axis(w[..., c1 + DECAY_LORA:c1 + DECAY_LORA + ICLR_LORA], -1, AD_PAD),
            _pad_axis(w[..., c1 + DECAY_LORA + ICLR_LORA:c_rw], -1, GD_PAD),
            w[..., c_rw:]], axis=-1)

    win_p = pad_cols(w_in).astype(BF16)
    mu_p = _row(pad_cols(rwkv_mu))
    wup_p = _pad_axis(rwkv_w_up, 1, WD_PAD).astype(BF16)
    aup_p = _pad_axis(rwkv_a_up, 1, AD_PAD).astype(BF16)
    gup_p = _pad_axis(rwkv_g_up, 1, GD_PAD).astype(BF16)
    head_of = jnp.arange(D_RWKV) // HEAD_DIM
    ones_bd = (head_of[:, None] == head_of[None, :]).astype(BF16)
    tok = jnp.arange(ts)
    tri = ((tok[:, None] // CHUNK == tok[None, :] // CHUNK)
           & (tok[None, :] <= tok[:, None])).astype(BF16)

    params = pltpu.CompilerParams(dimension_semantics=("arbitrary", "arbitrary"),
                                  vmem_limit_bytes=VMEM_LIMIT_BYTES)
    tile_spec = pl.BlockSpec((None, ts, D_MODEL), lambda b, s: (b, s, 0))
    shared2 = lambda a: pl.BlockSpec(a.shape, lambda b, s: (0, 0), pipeline_mode=pl.Buffered(1))

    wkv_b = xattn_w_kv.astype(BF16)
    wo_b = rwkv_w_o.astype(BF16)
    cwo_b = conv_w_o.astype(BF16)
    xwo_b = xattn_w_o.astype(BF16)
    wout_b = w_out.astype(BF16)
    fup_b = ffn_w_up.astype(BF16)
    fdn_b = ffn_w_down.astype(BF16)
    fg = final_norm_g.reshape(1, D_MODEL)

    for layer in range(depth):
        cs = functools.partial(_const_spec, layer=layer)
        kt, vb = pl.pallas_call(
            _kv_kernel,
            grid=(batch,),
            in_specs=[pl.BlockSpec((None, N_MEM, D_MODEL), lambda b: (b, 0, 0)),
                      pl.BlockSpec((None, 1, D_MODEL), lambda b: (layer, 0, 0)),
                      pl.BlockSpec((None, D_MODEL, 2 * D_XATTN), lambda b: (layer, 0, 0))],
            out_specs=[pl.BlockSpec((None, D_XATTN, XATTN_HEADS * N_MEM), lambda b: (b, 0, 0)),
                       pl.BlockSpec((None, XATTN_HEADS * N_MEM, D_XATTN), lambda b: (b, 0, 0))],
            out_shape=[jax.ShapeDtypeStruct((batch, D_XATTN, XATTN_HEADS * N_MEM), BF16),
                       jax.ShapeDtypeStruct((batch, XATTN_HEADS * N_MEM, D_XATTN), BF16)],
            name=f"xattn_kv_l{layer}",
        )(mem, _row(mem_norm_g), wkv_b)

        mixer_in = [
            (x, tile_spec),
            (kt, pl.BlockSpec((None, D_XATTN, XATTN_HEADS * N_MEM), lambda b, s: (b, 0, 0))),
            (vb, pl.BlockSpec((None, XATTN_HEADS * N_MEM, D_XATTN), lambda b, s: (b, 0, 0))),
        ]
        for arr in (_row(attn_norm_g), win_p, _row(b_gate), mu_p, _row(rwkv_w0), wup_p,
                    _row(rwkv_a0), aup_p, gup_p, _row(rwkv_k_k), _row(rwkv_k_a),
                    _row(rwkv_r_k.reshape(depth, D_RWKV)), _row(rwkv_gn_w), _row(rwkv_gn_b),
                    wo_b, conv_w_dw, _row(conv_b_dw), _row(conv_ln_g), _row(conv_ln_b), cwo_b,
                    _row(conv_b_o), xwo_b, wout_b):
            mixer_in.append((arr, cs(arr.shape)))
        mixer_in += [(ones_bd, shared2(ones_bd)), (tri, shared2(tri))]
        x = pl.pallas_call(
            _mixer_kernel,
            grid=(batch, n_seq),
            in_specs=[spec for _, spec in mixer_in],
            out_specs=tile_spec,
            out_shape=jax.ShapeDtypeStruct(x.shape, F32),
            scratch_shapes=[pltpu.VMEM((SUBLANES, C_CONV), F32),
                            pltpu.VMEM((HIST, D_CONV), F32),
                            pltpu.VMEM((N_PAIR, PAIR, PAIR), F32),
                            pltpu.VMEM((ts, D_RWKV), F32),
                            pltpu.VMEM((HIST + ts, D_CONV), F32)],
            compiler_params=params,
            name=f"mixer_l{layer}",
        )(*[arr for arr, _ in mixer_in])

        ffn_in = [(x, tile_spec)]
        for arr in (_row(ffn_norm_g), fup_b, ffn_w_dw, _row(ffn_b_dw), fdn_b):
            ffn_in.append((arr, cs(arr.shape)))
        ffn_in.append((fg, shared2(fg)))
        x = pl.pallas_call(
            functools.partial(_ffn_kernel, final_norm=(layer == depth - 1)),
            grid=(batch, n_seq),
            in_specs=[spec for _, spec in ffn_in],
            out_specs=tile_spec,
            out_shape=jax.ShapeDtypeStruct(x.shape, F32),
            scratch_shapes=[pltpu.VMEM((SUBLANES, 2 * D_FF), F32)],
            compiler_params=params,
            name=f"ffn_l{layer}",
        )(*[arr for arr, _ in ffn_in])
    return x
```

```python
import functools

import jax
import jax.numpy as jnp
from jax import lax
from jax.experimental import pallas as pl
from jax.experimental.pallas import tpu as pltpu

F32 = jnp.float32
BF16 = jnp.bfloat16

D_MODEL = 1024
N_MEM = 256
RWKV_HEADS = 8
HEAD_DIM = 64
D_RWKV = RWKV_HEADS * HEAD_DIM
DECAY_LORA = 64
ICLR_LORA = 64
GATE_LORA = 160
D_CONV = 256
CONV_WIDTH = 31
XATTN_HEADS = 4
D_XATTN = XATTN_HEADS * HEAD_DIM
N_BRANCH = 3
D_FF = 2816
FFN_CONV_WIDTH = 3
RMS_EPS = 1e-6
LN_EPS = 1e-5
GN_EPS = 64e-5

LANES = 128
SUBLANES = 8
CHUNK = 64
PAIR = 2 * HEAD_DIM
N_PAIR = D_RWKV // PAIR

WD_PAD = LANES
AD_PAD = LANES
GD_PAD = 2 * LANES
C_WD = 3 * D_RWKV
C_AD = C_WD + WD_PAD
C_GD = C_AD + AD_PAD
C_CONV = C_GD + GD_PAD
C_Q = C_CONV + 2 * D_CONV
C_GATE = C_Q + D_XATTN
N_IN_PAD = C_GATE + N_BRANCH * D_MODEL

HIST = 32
FF_BLOCK = 256
SEQ_TILE = 256
VMEM_LIMIT_BYTES = 56 * 1024 * 1024


def _dot(a, b):
    return jnp.dot(a, b, preferred_element_type=F32)


def _sigmoid(x):
    return 1.0 / (1.0 + jnp.exp(-x))


def _split2(x):
    hi = x.astype(BF16)
    return hi, (x - hi.astype(F32)).astype(BF16)


def _cumsum_dot(tri_bf16, x):
    hi, lo = _split2(x)
    return _dot(tri_bf16, hi) + _dot(tri_bf16, lo)


def _bdot(a, b):
    return lax.dot_general(a, b, (((2,), (1,)), ((0,), (0,))), preferred_element_type=F32)


def _bdot_nt(a, b):
    return lax.dot_general(a, b, (((2,), (2,)), ((0,), (0,))), preferred_element_type=F32)


def _two(x):
    lane = lax.broadcasted_iota(jnp.int32, x.shape, 2)
    first = lane < HEAD_DIM
    return jnp.concatenate([jnp.where(first, x, 0.0), jnp.where(first, 0.0, x)], axis=1)


def _blocks(x, n_chunk):
    return jnp.stack([x[c * CHUNK:(c + 1) * CHUNK, q * PAIR:(q + 1) * PAIR]
                      for c in range(n_chunk) for q in range(N_PAIR)])


def _core_scores(ah, rh, bh, kh, n_chunk):
    a_b, r_b, b_b, k_b = (_blocks(t, n_chunk) for t in (ah, rh, bh, kh))
    t_idx = lax.broadcasted_iota(jnp.int32, (1, CHUNK, PAIR), 1)
    j_idx = lax.broadcasted_iota(jnp.int32, (1, CHUNK, PAIR), 2) & (HEAD_DIM - 1)
    strict = j_idx < t_idx
    incl = j_idx <= t_idx
    sc = _bdot_nt(jnp.concatenate([a_b, r_b], axis=1),
                  jnp.concatenate([_two(b_b), _two(k_b)], axis=1))
    m_ab = jnp.where(strict, sc[:, 0:CHUNK, 0:PAIR], 0.0)
    m_ak = jnp.where(strict, sc[:, 0:CHUNK, PAIR:2 * PAIR], 0.0)
    n_rb = jnp.where(incl, sc[:, CHUNK:2 * CHUNK, 0:PAIR], 0.0)
    n_rk = jnp.where(incl, sc[:, CHUNK:2 * CHUNK, PAIR:2 * PAIR], 0.0)
    return a_b, r_b, m_ab, m_ak, n_rb, n_rk


def _core_inverse(m_ab):
    t_idx = lax.broadcasted_iota(jnp.int32, (1, CHUNK, PAIR), 1)
    j_idx = lax.broadcasted_iota(jnp.int32, (1, CHUNK, PAIR), 2) & (HEAD_DIM - 1)
    t_inv = jnp.where(j_idx == t_idx, 1.0, 0.0) + m_ab
    m_pow = _bdot(m_ab, _two(m_ab))
    for _ in range(4):
        res = _bdot(m_pow, jnp.concatenate([_two(m_pow), _two(t_inv)], axis=2))
        m_pow = res[:, :, 0:PAIR]
        t_inv = t_inv + res[:, :, PAIR:2 * PAIR]
    return t_inv + _bdot(m_pow, _two(t_inv))


def _core_transfer(t_inv, a_b, r_b, m_ak, n_rb, n_rk, v, b_end, k_end, pl_row, n_chunk):
    v_b = _blocks(v, n_chunk)
    nb = v_b.shape[0]
    makv = _bdot(m_ak, _two(v_b))
    res = _bdot(t_inv, jnp.concatenate([_two(a_b), _two(makv)], axis=2))
    w_til = res[:, :, 0:PAIR]
    u_til = res[:, :, PAIR:2 * PAIR]

    bk_end = jnp.concatenate([_blocks(b_end, n_chunk), _blocks(k_end, n_chunk)], axis=1)
    bk_end_t = jnp.stack([bk_end[i].T for i in range(nb)])
    ch = _bdot(bk_end_t, jnp.concatenate(
        [jnp.concatenate([w_til, u_til], axis=2),
         jnp.concatenate([jnp.zeros_like(v_b), v_b], axis=2)], axis=1))
    r_idx = lax.broadcasted_iota(jnp.int32, (1, PAIR, PAIR), 1)
    c_idx = lax.broadcasted_iota(jnp.int32, (1, PAIR, PAIR), 2)
    same_head = (r_idx < HEAD_DIM) == (c_idx < HEAD_DIM)
    c_mat = jnp.where(same_head, ch[:, :, 0:PAIR], 0.0)
    h_mat = jnp.where(same_head, ch[:, :, PAIR:2 * PAIR], 0.0)
    pl_b = jnp.stack([pl_row[c * CHUNK:c * CHUNK + 1, q * PAIR:(q + 1) * PAIR]
                      for c in range(n_chunk) for q in range(N_PAIR)])
    pl_col = jnp.sum(jnp.where(r_idx == c_idx, jnp.broadcast_to(pl_b, (nb, PAIR, PAIR)), 0.0),
                     axis=2, keepdims=True)

    v2 = _two(v_b)
    qy = _bdot(jnp.concatenate([n_rb, n_rk], axis=2), jnp.concatenate(
        [jnp.concatenate([_two(w_til), _two(u_til)], axis=2),
         jnp.concatenate([jnp.zeros_like(v2), v2], axis=2)], axis=1))
    return c_mat, h_mat, pl_col, r_b + qy[:, :, 0:PAIR], qy[:, :, PAIR:2 * PAIR]


def _core_scan(c_mat, h_mat, pl_col, q_mat, y0, state, n_chunk):
    starts = []
    for c in range(n_chunk):
        bs = slice(c * N_PAIR, (c + 1) * N_PAIR)
        starts.append(state)
        state = pl_col[bs] * state + _bdot(c_mat[bs], state) + h_mat[bs]
    y_b = _bdot(q_mat, jnp.concatenate(starts, axis=0)) + y0
    y = jnp.concatenate(
        [jnp.concatenate([y_b[c * N_PAIR + q] for q in range(N_PAIR)], axis=1)
         for c in range(n_chunk)], axis=0)
    return y, state


def _mixer_kernel(x_ref, kt_ref, vb_ref, g_ref, win_ref, bg_ref, mu_ref, w0_ref, wup_ref,
                  a0_ref, aup_ref, gup_ref, kkw_ref, ka_ref, rk_ref, gnw_ref, gnb_ref, wo_ref,
                  cw_ref, cb_ref, lng_ref, lnb_ref, cwo_ref, cbo_ref, xwo_ref, wout_ref,
                  ones_ref, tri_ref, o_ref,
                  prev_ref, hist_ref, state_ref, ubuf_ref, shift_ref):
    ts = x_ref.shape[0]
    n_chunk = ts // CHUNK

    @pl.when(pl.program_id(1) == 0)
    def _():
        prev_ref[...] = jnp.zeros_like(prev_ref)
        hist_ref[...] = jnp.zeros_like(hist_ref)
        state_ref[...] = jnp.zeros_like(state_ref)

    def gate_pre(j):
        lo = C_GATE + j * D_MODEL
        return _dot(hb, win_ref[:, lo:lo + D_MODEL]) + bg_ref[:, j * D_MODEL:(j + 1) * D_MODEL]

    x = x_ref[...]
    ms = jnp.mean(x * x, axis=-1, keepdims=True)
    hb = ((x * lax.rsqrt(ms + RMS_EPS)) * g_ref[...]).astype(BF16)
    row = lax.broadcasted_iota(jnp.int32, (ts, 1), 0)

    p = _dot(hb, win_ref[:, 0:C_CONV])
    pc = _dot(hb, win_ref[:, C_CONV:C_Q])
    qv = _dot(hb, win_ref[:, C_Q:C_GATE])

    shifted = jnp.where(row == 0, prev_ref[0:1, :], pltpu.roll(p, 1, 0))
    prev_ref[0:1, :] = p[ts - 1:ts, :]
    p = p + mu_ref[...] * (shifted - p)
    gate_a = gate_pre(0)
    r = p[:, 0:D_RWKV]
    k = p[:, D_RWKV:2 * D_RWKV]
    v = p[:, 2 * D_RWKV:3 * D_RWKV]
    wd = p[:, C_WD:C_AD]
    ad = p[:, C_AD:C_GD]
    gd = p[:, C_GD:C_CONV]

    z = w0_ref[...] + _dot(jnp.tanh(wd).astype(BF16), wup_ref[...])
    lw = (-jnp.exp(F32(-0.5))) * _sigmoid(z)
    a = _sigmoid(a0_ref[...] + _dot(ad.astype(BF16), aup_ref[...]))
    g_out = _dot(_sigmoid(gd).astype(BF16), gup_ref[...])

    ones_bd = ones_ref[...]
    kk = k * kkw_ref[...]
    ssq = _dot((kk * kk).astype(BF16), ones_bd)
    kk = kk / jnp.maximum(jnp.sqrt(ssq), 1e-12)
    k = k * (1.0 + (a - 1.0) * ka_ref[...])
    gate_b = gate_pre(1)

    cum = _cumsum_dot(tri_ref[...], lw)
    cum_end = jnp.concatenate(
        [jnp.broadcast_to(cum[(c + 1) * CHUNK - 1:(c + 1) * CHUNK, :], (CHUNK, D_RWKV))
         for c in range(n_chunk)], axis=0)
    e_inv = jnp.exp(-cum)
    e_end = jnp.exp(cum_end - cum)
    kka = kk * a
    rh_all = r * jnp.exp(cum)
    ah_all = -kk * jnp.exp(cum - lw)
    bh_all = kka * e_inv
    kh_all = k * e_inv
    pl_all = jnp.exp(cum_end)

    gate_c = gate_pre(2)
    sc = _dot(qv.astype(BF16), kt_ref[...])
    a_b, r_b, m_ab, m_ak, n_rb, n_rk = _core_scores(ah_all, rh_all, bh_all, kh_all, n_chunk)

    u = pc[:, 0:D_CONV] * _sigmoid(pc[:, D_CONV:2 * D_CONV])
    ubuf_ref[0:HIST, :] = hist_ref[...]
    ubuf_ref[HIST:HIST + ts, :] = u
    hist_ref[...] = u[ts - HIST:ts, :]
    acc = jnp.broadcast_to(cb_ref[...], (ts, D_CONV))
    first = HIST - (CONV_WIDTH - 1)
    for phase in range(SUBLANES):
        taps = range(phase, CONV_WIDTH, SUBLANES)
        span = ts + SUBLANES * (len(taps) - 1)
        shift_ref[0:span, :] = ubuf_ref[first + phase:first + phase + span, :]
        for i, tap in enumerate(taps):
            acc = acc + cw_ref[tap:tap + 1, :] * shift_ref[SUBLANES * i:SUBLANES * i + ts, :]

    t_inv = _core_inverse(m_ab)

    mu_ln = jnp.mean(acc, axis=-1, keepdims=True)
    dev = acc - mu_ln
    var = jnp.mean(dev * dev, axis=-1, keepdims=True)
    u = dev * lax.rsqrt(var + LN_EPS) * lng_ref[...] + lnb_ref[...]
    u = u * _sigmoid(u)
    y_b = _dot(u.astype(BF16), cwo_ref[...]) + cbo_ref[...]

    c_mat, h_mat, pl_col, q_mat, y0 = _core_transfer(
        t_inv, a_b, r_b, m_ak, n_rb, n_rk, v, kka * e_end, k * e_end, pl_all, n_chunk)

    probs = []
    for h in range(XATTN_HEADS):
        s_h = sc[:, h * N_MEM:(h + 1) * N_MEM]
        e_h = jnp.exp(s_h - jnp.max(s_h, axis=-1, keepdims=True))
        probs.append(e_h / jnp.sum(e_h, axis=-1, keepdims=True))
    pr = jnp.concatenate(probs, axis=1).astype(BF16)

    y, new_state = _core_scan(c_mat, h_mat, pl_col, q_mat, y0, state_ref[...], n_chunk)
    state_ref[...] = new_state

    o_c = _dot(pr, vb_ref[...])
    y_c = _dot(o_c.astype(BF16), xwo_ref[...])
    merged = _sigmoid(gate_b) * y_b + _sigmoid(gate_c) * y_c

    inv_n = F32(1.0 / HEAD_DIM)
    mean = _dot(y.astype(BF16), ones_bd) * inv_n
    dev = y - mean
    var = _dot((dev * dev).astype(BF16), ones_bd) * inv_n
    y = dev * lax.rsqrt(var + GN_EPS) * gnw_ref[...] + gnb_ref[...]
    bonus = _dot((r * k * rk_ref[...]).astype(BF16), ones_bd) * v
    y_a = _dot(((y + bonus) * g_out).astype(BF16), wo_ref[...])
    merged = merged + _sigmoid(gate_a) * y_a

    o_ref[...] = x + _dot(merged.astype(BF16), wout_ref[...])


def _ffn_kernel(x_ref, g_ref, wup_ref, wdw_ref, bdw_ref, wdn_ref, fg_ref, o_ref, carry_ref,
                *, final_norm):
    ts = x_ref.shape[0]

    @pl.when(pl.program_id(1) == 0)
    def _():
        carry_ref[...] = jnp.zeros_like(carry_ref)

    x = x_ref[...]
    ms = jnp.mean(x * x, axis=-1, keepdims=True)
    hb = ((x * lax.rsqrt(ms + RMS_EPS)) * g_ref[...]).astype(BF16)
    row = lax.broadcasted_iota(jnp.int32, (ts, 1), 0)

    def up_block(col):
        return _dot(hb, wup_ref[:, col:col + FF_BLOCK])

    def conv_block(zb, col):
        cs = slice(col, col + FF_BLOCK)
        tail = carry_ref[:, cs]
        carry_ref[:, cs] = zb[ts - SUBLANES:ts, :]
        z1 = jnp.where(row == 0, tail[SUBLANES - 1:SUBLANES, :], pltpu.roll(zb, 1, 0))
        z2 = jnp.where(row == 0, tail[SUBLANES - 2:SUBLANES - 1, :],
                       jnp.where(row == 1, tail[SUBLANES - 1:SUBLANES, :], pltpu.roll(zb, 2, 0)))
        return (wdw_ref[0:1, cs] * z2 + wdw_ref[1:2, cs] * z1 + wdw_ref[2:3, cs] * zb
                + bdw_ref[:, cs])

    n_blk = D_FF // FF_BLOCK
    acc = jnp.zeros((ts, D_MODEL), F32)
    z_next = (up_block(0), up_block(D_FF))
    act_prev = None
    for j in range(n_blk):
        zu, zg = z_next
        if j + 1 < n_blk:
            z_next = (up_block((j + 1) * FF_BLOCK), up_block(D_FF + (j + 1) * FF_BLOCK))
        if act_prev is not None:
            acc = acc + _dot(act_prev, wdn_ref[(j - 1) * FF_BLOCK:j * FF_BLOCK, :])
        u = conv_block(zu, j * FF_BLOCK)
        gt = conv_block(zg, D_FF + j * FF_BLOCK)
        act_prev = (gt * _sigmoid(gt) * u).astype(BF16)
    acc = acc + _dot(act_prev, wdn_ref[(n_blk - 1) * FF_BLOCK:n_blk * FF_BLOCK, :])
    out = x + acc
    if final_norm:
        ms = jnp.mean(out * out, axis=-1, keepdims=True)
        out = out * lax.rsqrt(ms + RMS_EPS) * fg_ref[...]
    o_ref[...] = out


def _kv_kernel(mem_ref, g_ref, wkv_ref, kt_ref, vb_ref):
    m = mem_ref[...]
    ms = jnp.mean(m * m, axis=-1, keepdims=True)
    mb = ((m * lax.rsqrt(ms + RMS_EPS)) * g_ref[...]).astype(BF16)
    kv = _dot(mb, wkv_ref[...])
    k_t = kv[:, 0:D_XATTN].T * F32(HEAD_DIM ** -0.5)
    vv = kv[:, D_XATTN:2 * D_XATTN]
    r_idx = lax.broadcasted_iota(jnp.int32, (D_XATTN, XATTN_HEADS * N_MEM), 0)
    c_idx = lax.broadcasted_iota(jnp.int32, (D_XATTN, XATTN_HEADS * N_MEM), 1)
    kt_ref[...] = jnp.where(r_idx // HEAD_DIM == c_idx // N_MEM,
                            jnp.concatenate([k_t] * XATTN_HEADS, axis=1), 0.0).astype(BF16)
    r_idx = lax.broadcasted_iota(jnp.int32, (XATTN_HEADS * N_MEM, D_XATTN), 0)
    c_idx = lax.broadcasted_iota(jnp.int32, (XATTN_HEADS * N_MEM, D_XATTN), 1)
    vb_ref[...] = jnp.where(r_idx // N_MEM == c_idx // HEAD_DIM,
                            jnp.concatenate([vv] * XATTN_HEADS, axis=0), 0.0).astype(BF16)


def _const_spec(shape, layer):
    nd = len(shape)
    return pl.BlockSpec((None,) + tuple(shape[1:]), lambda b, s: (layer,) + (0,) * (nd - 1),
                        pipeline_mode=pl.Buffered(1))


def _pad_axis(x, axis, new):
    pad = [(0, 0)] * x.ndim
    pad[axis] = (0, new - x.shape[axis])
    return jnp.pad(x, pad)


def _row(x):
    return x.reshape(x.shape[0], 1, -1)


def kernel(x, mem, attn_norm_g, w_in, b_gate, rwkv_mu, rwkv_w0, rwkv_w_up, rwkv_a0,
           rwkv_a_up, rwkv_g_up, rwkv_k_k, rwkv_k_a, rwkv_r_k, rwkv_gn_w, rwkv_gn_b,
           rwkv_w_o, conv_w_dw, conv_b_dw, conv_ln_g, conv_ln_b, conv_w_o, conv_b_o,
           mem_norm_g, xattn_w_kv, xattn_w_o, w_out, ffn_norm_g, ffn_w_up, ffn_w_dw,
           ffn_b_dw, ffn_w_down, final_norm_g):
    batch, seq, _ = x.shape
    depth = w_in.shape[0]
    ts = min(SEQ_TILE, seq)
    assert seq % ts == 0 and ts % CHUNK == 0 and ts >= HIST
    n_seq = seq // ts
    c1 = 3 * D_RWKV
    c_rw = c1 + DECAY_LORA + ICLR_LORA + GATE_LORA

    def pad_cols(w):
        return jnp.concatenate([
            w[..., 0:c1],
            _pad_axis(w[..., c1:c1 + DECAY_LORA], -1, WD_PAD),
            _pad_axis(w[..., c1 + DECAY_LORA:c1 + DECAY_LORA + ICLR_LORA], -1, AD_PAD),
            _pad_axis(w[..., c1 + DECAY_LORA + ICLR_LORA:c_rw], -1, GD_PAD),
            w[..., c_rw:]], axis=-1)

    win_p = pad_cols(w_in).astype(BF16)
    mu_p = _row(pad_cols(rwkv_mu))
    wup_p = _pad_axis(rwkv_w_up, 1, WD_PAD).astype(BF16)
    aup_p = _pad_axis(rwkv_a_up, 1, AD_PAD).astype(BF16)
    gup_p = _pad_axis(rwkv_g_up, 1, GD_PAD).astype(BF16)
    head_of = jnp.arange(D_RWKV) // HEAD_DIM
    ones_bd = (head_of[:, None] == head_of[None, :]).astype(BF16)
    tok = jnp.arange(ts)
    tri = ((tok[:, None] // CHUNK == tok[None, :] // CHUNK)
           & (tok[None, :] <= tok[:, None])).astype(BF16)

    params = pltpu.CompilerParams(dimension_semantics=("arbitrary", "arbitrary"),
                                  vmem_limit_bytes=VMEM_LIMIT_BYTES)
    tile_spec = pl.BlockSpec((None, ts, D_MODEL), lambda b, s: (b, s, 0))
    shared2 = lambda a: pl.BlockSpec(a.shape, lambda b, s: (0, 0), pipeline_mode=pl.Buffered(1))

    wkv_b = xattn_w_kv.astype(BF16)
    wo_b = rwkv_w_o.astype(BF16)
    cwo_b = conv_w_o.astype(BF16)
    xwo_b = xattn_w_o.astype(BF16)
    wout_b = w_out.astype(BF16)
    fup_b = ffn_w_up.astype(BF16)
    fdn_b = ffn_w_down.astype(BF16)
    fg = final_norm_g.reshape(1, D_MODEL)

    for layer in range(depth):
        cs = functools.partial(_const_spec, layer=layer)
        kt, vb = pl.pallas_call(
            _kv_kernel,
            grid=(batch,),
            in_specs=[pl.BlockSpec((None, N_MEM, D_MODEL), lambda b: (b, 0, 0)),
                      pl.BlockSpec((None, 1, D_MODEL), lambda b: (layer, 0, 0)),
                      pl.BlockSpec((None, D_MODEL, 2 * D_XATTN), lambda b: (layer, 0, 0))],
            out_specs=[pl.BlockSpec((None, D_XATTN, XATTN_HEADS * N_MEM), lambda b: (b, 0, 0)),
                       pl.BlockSpec((None, XATTN_HEADS * N_MEM, D_XATTN), lambda b: (b, 0, 0))],
            out_shape=[jax.ShapeDtypeStruct((batch, D_XATTN, XATTN_HEADS * N_MEM), BF16),
                       jax.ShapeDtypeStruct((batch, XATTN_HEADS * N_MEM, D_XATTN), BF16)],
            name=f"xattn_kv_l{layer}",
        )(mem, _row(mem_norm_g), wkv_b)

        mixer_in = [
            (x, tile_spec),
            (kt, pl.BlockSpec((None, D_XATTN, XATTN_HEADS * N_MEM), lambda b, s: (b, 0, 0))),
            (vb, pl.BlockSpec((None, XATTN_HEADS * N_MEM, D_XATTN), lambda b, s: (b, 0, 0))),
        ]
        for arr in (_row(attn_norm_g), win_p, _row(b_gate), mu_p, _row(rwkv_w0), wup_p,
                    _row(rwkv_a0), aup_p, gup_p, _row(rwkv_k_k), _row(rwkv_k_a),
                    _row(rwkv_r_k.reshape(depth, D_RWKV)), _row(rwkv_gn_w), _row(rwkv_gn_b),
                    wo_b, conv_w_dw, _row(conv_b_dw), _row(conv_ln_g), _row(conv_ln_b), cwo_b,
                    _row(conv_b_o), xwo_b, wout_b):
            mixer_in.append((arr, cs(arr.shape)))
        mixer_in += [(ones_bd, shared2(ones_bd)), (tri, shared2(tri))]
        x = pl.pallas_call(
            _mixer_kernel,
            grid=(batch, n_seq),
            in_specs=[spec for _, spec in mixer_in],
            out_specs=tile_spec,
            out_shape=jax.ShapeDtypeStruct(x.shape, F32),
            scratch_shapes=[pltpu.VMEM((SUBLANES, C_CONV), F32),
                            pltpu.VMEM((HIST, D_CONV), F32),
                            pltpu.VMEM((N_PAIR, PAIR, PAIR), F32),
                            pltpu.VMEM((HIST + ts, D_CONV), F32),
                            pltpu.VMEM((HIST + ts, D_CONV), F32)],
            compiler_params=params,
            name=f"mixer_l{layer}",
        )(*[arr for arr, _ in mixer_in])

        ffn_in = [(x, tile_spec)]
        for arr in (_row(ffn_norm_g), fup_b, ffn_w_dw, _row(ffn_b_dw), fdn_b):
            ffn_in.append((arr, cs(arr.shape)))
        ffn_in.append((fg, shared2(fg)))
        x = pl.pallas_call(
            functools.partial(_ffn_kernel, final_norm=(layer == depth - 1)),
            grid=(batch, n_seq),
            in_specs=[spec for _, spec in ffn_in],
            out_specs=tile_spec,
            out_shape=jax.ShapeDtypeStruct(x.shape, F32),
            scratch_shapes=[pltpu.VMEM((SUBLANES, 2 * D_FF), F32)],
            compiler_params=params,
            name=f"ffn_l{layer}",
        )(*[arr for arr, _ in ffn_in])
    return x
```

```python
import functools

import jax
import jax.numpy as jnp
from jax import lax
from jax.experimental import pallas as pl
from jax.experimental.pallas import tpu as pltpu

F32 = jnp.float32
BF16 = jnp.bfloat16

D_MODEL = 1024
N_MEM = 256
RWKV_HEADS = 8
HEAD_DIM = 64
D_RWKV = RWKV_HEADS * HEAD_DIM
DECAY_LORA = 64
ICLR_LORA = 64
GATE_LORA = 160
D_CONV = 256
CONV_WIDTH = 31
XATTN_HEADS = 4
D_XATTN = XATTN_HEADS * HEAD_DIM
N_BRANCH = 3
D_FF = 2816
FFN_CONV_WIDTH = 3
RMS_EPS = 1e-6
LN_EPS = 1e-5
GN_EPS = 64e-5

LANES = 128
SUBLANES = 8
CHUNK = 64
PAIR = 2 * HEAD_DIM
N_PAIR = D_RWKV // PAIR

WD_PAD = LANES
AD_PAD = LANES
GD_PAD = 2 * LANES
C_WD = 3 * D_RWKV
C_AD = C_WD + WD_PAD
C_GD = C_AD + AD_PAD
C_CONV = C_GD + GD_PAD
C_Q = C_CONV + 2 * D_CONV
C_GATE = C_Q + D_XATTN
N_IN_PAD = C_GATE + N_BRANCH * D_MODEL

HIST = 32
FF_BLOCK = 256
MIXER_TILE = 512
FFN_TILE = 256
VMEM_LIMIT_BYTES = 56 * 1024 * 1024


def _dot(a, b):
    return jnp.dot(a, b, preferred_element_type=F32)


def _sigmoid(x):
    return 0.5 * jnp.tanh(0.5 * x) + 0.5


def _split2(x):
    hi = x.astype(BF16)
    return hi, (x - hi.astype(F32)).astype(BF16)


def _cumsum_dot(tri_bf16, x):
    hi, lo = _split2(x)
    return _dot(tri_bf16, hi) + _dot(tri_bf16, lo)


def _head_sum(x, ones_half):
    xb = x.astype(BF16)
    half = ones_half.shape[0]
    return jnp.concatenate([_dot(xb[:, 0:half], ones_half), _dot(xb[:, half:], ones_half)],
                           axis=1)


def _bdot(a, b):
    return lax.dot_general(a, b, (((2,), (1,)), ((0,), (0,))), preferred_element_type=F32)


def _bdot_nt(a, b):
    return lax.dot_general(a, b, (((2,), (2,)), ((0,), (0,))), preferred_element_type=F32)


def _two(x):
    lane = lax.broadcasted_iota(jnp.int32, x.shape, 2)
    first = lane < HEAD_DIM
    return jnp.concatenate([jnp.where(first, x, 0.0), jnp.where(first, 0.0, x)], axis=1)


def _blocks(x, n_chunk):
    return jnp.stack([x[c * CHUNK:(c + 1) * CHUNK, q * PAIR:(q + 1) * PAIR]
                      for c in range(n_chunk) for q in range(N_PAIR)])


def _core_scores(ah, rh, bh, kh, n_chunk):
    a_b, r_b, b_b, k_b = (_blocks(t, n_chunk) for t in (ah, rh, bh, kh))
    t_idx = lax.broadcasted_iota(jnp.int32, (1, CHUNK, PAIR), 1)
    j_idx = lax.broadcasted_iota(jnp.int32, (1, CHUNK, PAIR), 2) & (HEAD_DIM - 1)
    strict = j_idx < t_idx
    incl = j_idx <= t_idx
    sc = _bdot_nt(jnp.concatenate([a_b, r_b], axis=1),
                  jnp.concatenate([_two(b_b), _two(k_b)], axis=1))
    m_ab = jnp.where(strict, sc[:, 0:CHUNK, 0:PAIR], 0.0)
    m_ak = jnp.where(strict, sc[:, 0:CHUNK, PAIR:2 * PAIR], 0.0)
    n_rb = jnp.where(incl, sc[:, CHUNK:2 * CHUNK, 0:PAIR], 0.0)
    n_rk = jnp.where(incl, sc[:, CHUNK:2 * CHUNK, PAIR:2 * PAIR], 0.0)
    return a_b, r_b, m_ab, m_ak, n_rb, n_rk


def _core_inverse(m_ab):
    t_idx = lax.broadcasted_iota(jnp.int32, (1, CHUNK, PAIR), 1)
    j_idx = lax.broadcasted_iota(jnp.int32, (1, CHUNK, PAIR), 2) & (HEAD_DIM - 1)
    t_inv = jnp.where(j_idx == t_idx, 1.0, 0.0) + m_ab
    m_pow = _bdot(m_ab, _two(m_ab))
    for _ in range(4):
        res = _bdot(m_pow, jnp.concatenate([_two(m_pow), _two(t_inv)], axis=2))
        m_pow = res[:, :, 0:PAIR]
        t_inv = t_inv + res[:, :, PAIR:2 * PAIR]
    return t_inv + _bdot(m_pow, _two(t_inv))


def _core_transfer(t_inv, a_b, r_b, m_ak, n_rb, n_rk, v, b_end, k_end, pl_row, n_chunk):
    v_b = _blocks(v, n_chunk)
    nb = v_b.shape[0]
    makv = _bdot(m_ak, _two(v_b))
    res = _bdot(t_inv, jnp.concatenate([_two(a_b), _two(makv)], axis=2))
    w_til = res[:, :, 0:PAIR]
    u_til = res[:, :, PAIR:2 * PAIR]

    bk_end = jnp.concatenate([_blocks(b_end, n_chunk), _blocks(k_end, n_chunk)], axis=1)
    bk_end_t = jnp.stack([bk_end[i].T for i in range(nb)])
    ch = _bdot(bk_end_t, jnp.concatenate(
        [jnp.concatenate([w_til, u_til], axis=2),
         jnp.concatenate([jnp.zeros_like(v_b), v_b], axis=2)], axis=1))
    r_idx = lax.broadcasted_iota(jnp.int32, (1, PAIR, PAIR), 1)
    c_idx = lax.broadcasted_iota(jnp.int32, (1, PAIR, PAIR), 2)
    same_head = (r_idx < HEAD_DIM) == (c_idx < HEAD_DIM)
    c_mat = jnp.where(same_head, ch[:, :, 0:PAIR], 0.0)
    h_mat = jnp.where(same_head, ch[:, :, PAIR:2 * PAIR], 0.0)
    pl_b = jnp.stack([pl_row[c * CHUNK:c * CHUNK + 1, q * PAIR:(q + 1) * PAIR]
                      for c in range(n_chunk) for q in range(N_PAIR)])
    pl_col = jnp.sum(jnp.where(r_idx == c_idx, jnp.broadcast_to(pl_b, (nb, PAIR, PAIR)), 0.0),
                     axis=2, keepdims=True)

    v2 = _two(v_b)
    qy = _bdot(jnp.concatenate([n_rb, n_rk], axis=2), jnp.concatenate(
        [jnp.concatenate([_two(w_til), _two(u_til)], axis=2),
         jnp.concatenate([jnp.zeros_like(v2), v2], axis=2)], axis=1))
    return c_mat, h_mat, pl_col, r_b + qy[:, :, 0:PAIR], qy[:, :, PAIR:2 * PAIR]


def _core_scan(c_mat, h_mat, pl_col, q_mat, y0, state, n_chunk):
    starts = []
    for c in range(n_chunk):
        bs = slice(c * N_PAIR, (c + 1) * N_PAIR)
        starts.append(state)
        state = pl_col[bs] * state + _bdot(c_mat[bs], state) + h_mat[bs]
    y_b = _bdot(q_mat, jnp.concatenate(starts, axis=0)) + y0
    y = jnp.concatenate(
        [jnp.concatenate([y_b[c * N_PAIR + q] for q in range(N_PAIR)], axis=1)
         for c in range(n_chunk)], axis=0)
    return y, state


def _mixer_kernel(x_ref, kt_ref, vb_ref, g_ref, win_ref, bg_ref, mu_ref, w0_ref, wup_ref,
                  a0_ref, aup_ref, gup_ref, kkw_ref, ka_ref, rk_ref, gnw_ref, gnb_ref, wo_ref,
                  cw_ref, cb_ref, lng_ref, lnb_ref, cwo_ref, cbo_ref, xwo_ref, wout_ref,
                  ones_ref, tri_ref, o_ref,
                  prev_ref, hist_ref, state_ref, ubuf_ref, shift_ref):
    ts = x_ref.shape[0]
    n_chunk = ts // CHUNK

    @pl.when(pl.program_id(1) == 0)
    def _():
        prev_ref[...] = jnp.zeros_like(prev_ref)
        hist_ref[...] = jnp.zeros_like(hist_ref)
        state_ref[...] = jnp.zeros_like(state_ref)

    def gate_pre(j):
        lo = C_GATE + j * D_MODEL
        return _dot(hb, win_ref[:, lo:lo + D_MODEL]) + bg_ref[:, j * D_MODEL:(j + 1) * D_MODEL]

    x = x_ref[...]
    ms = jnp.mean(x * x, axis=-1, keepdims=True)
    hb = ((x * lax.rsqrt(ms + RMS_EPS)) * g_ref[...]).astype(BF16)
    row = lax.broadcasted_iota(jnp.int32, (ts, 1), 0)

    p = _dot(hb, win_ref[:, 0:C_CONV])
    pc = _dot(hb, win_ref[:, C_CONV:C_Q])
    qv = _dot(hb, win_ref[:, C_Q:C_GATE])

    shifted = jnp.where(row == 0, prev_ref[0:1, :], pltpu.roll(p, 1, 0))
    prev_ref[0:1, :] = p[ts - 1:ts, :]
    p = p + mu_ref[...] * (shifted - p)
    gate_a = gate_pre(0)
    r = p[:, 0:D_RWKV]
    k = p[:, D_RWKV:2 * D_RWKV]
    v = p[:, 2 * D_RWKV:3 * D_RWKV]
    wd = p[:, C_WD:C_AD]
    ad = p[:, C_AD:C_GD]
    gd = p[:, C_GD:C_CONV]

    z = w0_ref[...] + _dot(jnp.tanh(wd).astype(BF16), wup_ref[...])
    lw = (-jnp.exp(F32(-0.5))) * _sigmoid(z)
    a = _sigmoid(a0_ref[...] + _dot(ad.astype(BF16), aup_ref[...]))
    g_out = _dot(_sigmoid(gd).astype(BF16), gup_ref[...])

    ones_bd = ones_ref[...]
    kk = k * kkw_ref[...]
    ssq = _head_sum(kk * kk, ones_bd)
    kk = kk / jnp.maximum(jnp.sqrt(ssq), 1e-12)
    k = k * (1.0 + (a - 1.0) * ka_ref[...])
    gate_b = gate_pre(1)

    cum = _cumsum_dot(tri_ref[...], lw)
    cum_end = jnp.concatenate(
        [jnp.broadcast_to(cum[(c + 1) * CHUNK - 1:(c + 1) * CHUNK, :], (CHUNK, D_RWKV))
         for c in range(n_chunk)], axis=0)
    e_inv = jnp.exp(-cum)
    e_end = jnp.exp(cum_end - cum)
    kka = kk * a
    rh_all = r * jnp.exp(cum)
    ah_all = -kk * jnp.exp(cum - lw)
    bh_all = kka * e_inv
    kh_all = k * e_inv
    pl_all = jnp.exp(cum_end)

    gate_c = gate_pre(2)
    sc = _dot(qv.astype(BF16), kt_ref[...])
    a_b, r_b, m_ab, m_ak, n_rb, n_rk = _core_scores(ah_all, rh_all, bh_all, kh_all, n_chunk)

    u = pc[:, 0:D_CONV] * _sigmoid(pc[:, D_CONV:2 * D_CONV])
    ubuf_ref[0:HIST, :] = hist_ref[...]
    ubuf_ref[HIST:HIST + ts, :] = u
    hist_ref[...] = u[ts - HIST:ts, :]
    acc = jnp.broadcast_to(cb_ref[...], (ts, D_CONV))
    first = HIST - (CONV_WIDTH - 1)
    for phase in range(SUBLANES):
        taps = range(phase, CONV_WIDTH, SUBLANES)
        span = ts + SUBLANES * (len(taps) - 1)
        shift_ref[0:span, :] = ubuf_ref[first + phase:first + phase + span, :]
        for i, tap in enumerate(taps):
            acc = acc + cw_ref[tap:tap + 1, :] * shift_ref[SUBLANES * i:SUBLANES * i + ts, :]

    t_inv = _core_inverse(m_ab)

    mu_ln = jnp.mean(acc, axis=-1, keepdims=True)
    dev = acc - mu_ln
    var = jnp.mean(dev * dev, axis=-1, keepdims=True)
    u = dev * lax.rsqrt(var + LN_EPS) * lng_ref[...] + lnb_ref[...]
    u = u * _sigmoid(u)
    y_b = _dot(u.astype(BF16), cwo_ref[...]) + cbo_ref[...]

    c_mat, h_mat, pl_col, q_mat, y0 = _core_transfer(
        t_inv, a_b, r_b, m_ak, n_rb, n_rk, v, kka * e_end, k * e_end, pl_all, n_chunk)

    probs = []
    for h in range(XATTN_HEADS):
        s_h = sc[:, h * N_MEM:(h + 1) * N_MEM]
        e_h = jnp.exp(s_h - jnp.max(s_h, axis=-1, keepdims=True))
        probs.append(e_h / jnp.sum(e_h, axis=-1, keepdims=True))
    pr = jnp.concatenate(probs, axis=1).astype(BF16)

    y, new_state = _core_scan(c_mat, h_mat, pl_col, q_mat, y0, state_ref[...], n_chunk)
    state_ref[...] = new_state

    o_c = _dot(pr, vb_ref[...])
    y_c = _dot(o_c.astype(BF16), xwo_ref[...])
    merged = _sigmoid(gate_b) * y_b + _sigmoid(gate_c) * y_c

    inv_n = F32(1.0 / HEAD_DIM)
    mean = _head_sum(y, ones_bd) * inv_n
    dev = y - mean
    var = _head_sum(dev * dev, ones_bd) * inv_n
    y = dev * lax.rsqrt(var + GN_EPS) * gnw_ref[...] + gnb_ref[...]
    bonus = _head_sum(r * k * rk_ref[...], ones_bd) * v
    y_a = _dot(((y + bonus) * g_out).astype(BF16), wo_ref[...])
    merged = merged + _sigmoid(gate_a) * y_a

    o_ref[...] = x + _dot(merged.astype(BF16), wout_ref[...])


def _ffn_kernel(x_ref, g_ref, wup_ref, wdw_ref, bdw_ref, wdn_ref, fg_ref, o_ref, carry_ref,
                *, final_norm):
    ts = x_ref.shape[0]

    @pl.when(pl.program_id(1) == 0)
    def _():
        carry_ref[...] = jnp.zeros_like(carry_ref)

    x = x_ref[...]
    ms = jnp.mean(x * x, axis=-1, keepdims=True)
    hb = ((x * lax.rsqrt(ms + RMS_EPS)) * g_ref[...]).astype(BF16)
    row = lax.broadcasted_iota(jnp.int32, (ts, 1), 0)

    def up_block(col):
        return _dot(hb, wup_ref[:, col:col + FF_BLOCK])

    def conv_block(zb, col):
        cs = slice(col, col + FF_BLOCK)
        tail = carry_ref[:, cs]
        carry_ref[:, cs] = zb[ts - SUBLANES:ts, :]
        z1 = jnp.where(row == 0, tail[SUBLANES - 1:SUBLANES, :], pltpu.roll(zb, 1, 0))
        z2 = jnp.where(row == 0, tail[SUBLANES - 2:SUBLANES - 1, :],
                       jnp.where(row == 1, tail[SUBLANES - 1:SUBLANES, :], pltpu.roll(zb, 2, 0)))
        return (wdw_ref[0:1, cs] * z2 + wdw_ref[1:2, cs] * z1 + wdw_ref[2:3, cs] * zb
                + bdw_ref[:, cs])

    n_blk = D_FF // FF_BLOCK
    acc = jnp.zeros((ts, D_MODEL), F32)
    z_next = (up_block(0), up_block(D_FF))
    act_prev = None
    for j in range(n_blk):
        zu, zg = z_next
        if j + 1 < n_blk:
            z_next = (up_block((j + 1) * FF_BLOCK), up_block(D_FF + (j + 1) * FF_BLOCK))
        if act_prev is not None:
            acc = acc + _dot(act_prev, wdn_ref[(j - 1) * FF_BLOCK:j * FF_BLOCK, :])
        u = conv_block(zu, j * FF_BLOCK)
        gt = conv_block(zg, D_FF + j * FF_BLOCK)
        half = 0.5 * gt
        act_prev = ((half + half * jnp.tanh(half)) * u).astype(BF16)
    acc = acc + _dot(act_prev, wdn_ref[(n_blk - 1) * FF_BLOCK:n_blk * FF_BLOCK, :])
    out = x + acc
    if final_norm:
        ms = jnp.mean(out * out, axis=-1, keepdims=True)
        out = out * lax.rsqrt(ms + RMS_EPS) * fg_ref[...]
    o_ref[...] = out


def _kv_kernel(mem_ref, g_ref, wkv_ref, kt_ref, vb_ref):
    m = mem_ref[...]
    ms = jnp.mean(m * m, axis=-1, keepdims=True)
    mb = ((m * lax.rsqrt(ms + RMS_EPS)) * g_ref[...]).astype(BF16)
    kv = _dot(mb, wkv_ref[...])
    k_t = kv[:, 0:D_XATTN].T * F32(HEAD_DIM ** -0.5)
    vv = kv[:, D_XATTN:2 * D_XATTN]
    r_idx = lax.broadcasted_iota(jnp.int32, (D_XATTN, XATTN_HEADS * N_MEM), 0)
    c_idx = lax.broadcasted_iota(jnp.int32, (D_XATTN, XATTN_HEADS * N_MEM), 1)
    kt_ref[...] = jnp.where(r_idx // HEAD_DIM == c_idx // N_MEM,
                            jnp.concatenate([k_t] * XATTN_HEADS, axis=1), 0.0).astype(BF16)
    r_idx = lax.broadcasted_iota(jnp.int32, (XATTN_HEADS * N_MEM, D_XATTN), 0)
    c_idx = lax.broadcasted_iota(jnp.int32, (XATTN_HEADS * N_MEM, D_XATTN), 1)
    vb_ref[...] = jnp.where(r_idx // N_MEM == c_idx // HEAD_DIM,
                            jnp.concatenate([vv] * XATTN_HEADS, axis=0), 0.0).astype(BF16)


def _const_spec(shape, layer):
    nd = len(shape)
    return pl.BlockSpec((None,) + tuple(shape[1:]), lambda b, s: (layer,) + (0,) * (nd - 1),
                        pipeline_mode=pl.Buffered(1))


def _pad_axis(x, axis, new):
    pad = [(0, 0)] * x.ndim
    pad[axis] = (0, new - x.shape[axis])
    return jnp.pad(x, pad)


def _row(x):
    return x.reshape(x.shape[0], 1, -1)


def kernel(x, mem, attn_norm_g, w_in, b_gate, rwkv_mu, rwkv_w0, rwkv_w_up, rwkv_a0,
           rwkv_a_up, rwkv_g_up, rwkv_k_k, rwkv_k_a, rwkv_r_k, rwkv_gn_w, rwkv_gn_b,
           rwkv_w_o, conv_w_dw, conv_b_dw, conv_ln_g, conv_ln_b, conv_w_o, conv_b_o,
           mem_norm_g, xattn_w_kv, xattn_w_o, w_out, ffn_norm_g, ffn_w_up, ffn_w_dw,
           ffn_b_dw, ffn_w_down, final_norm_g):
    batch, seq, _ = x.shape
    depth = w_in.shape[0]
    ts = min(MIXER_TILE, seq)
    ts_ffn = min(FFN_TILE, seq)
    assert seq % ts == 0 and ts % CHUNK == 0 and ts >= HIST
    assert seq % ts_ffn == 0 and ts_ffn % SUBLANES == 0
    c1 = 3 * D_RWKV
    c_rw = c1 + DECAY_LORA + ICLR_LORA + GATE_LORA

    def pad_cols(w):
        return jnp.concatenate([
            w[..., 0:c1],
            _pad_axis(w[..., c1:c1 + DECAY_LORA], -1, WD_PAD),
            _pad_axis(w[..., c1 + DECAY_LORA:c1 + DECAY_LORA + ICLR_LORA], -1, AD_PAD),
            _pad_axis(w[..., c1 + DECAY_LORA + ICLR_LORA:c_rw], -1, GD_PAD),
            w[..., c_rw:]], axis=-1)

    win_p = pad_cols(w_in).astype(BF16)
    mu_p = _row(pad_cols(rwkv_mu))
    wup_p = _pad_axis(rwkv_w_up, 1, WD_PAD).astype(BF16)
    aup_p = _pad_axis(rwkv_a_up, 1, AD_PAD).astype(BF16)
    gup_p = _pad_axis(rwkv_g_up, 1, GD_PAD).astype(BF16)
    head_of = jnp.arange(D_RWKV // 2) // HEAD_DIM
    ones_bd = (head_of[:, None] == head_of[None, :]).astype(BF16)
    tok = jnp.arange(ts)
    tri = ((tok[:, None] // CHUNK == tok[None, :] // CHUNK)
           & (tok[None, :] <= tok[:, None])).astype(BF16)

    params = pltpu.CompilerParams(dimension_semantics=("arbitrary", "arbitrary"),
                                  vmem_limit_bytes=VMEM_LIMIT_BYTES)
    tile_spec = lambda rows: pl.BlockSpec((None, rows, D_MODEL), lambda b, s: (b, s, 0))
    shared2 = lambda a: pl.BlockSpec(a.shape, lambda b, s: (0, 0), pipeline_mode=pl.Buffered(1))

    wkv_b = xattn_w_kv.astype(BF16)
    wo_b = rwkv_w_o.astype(BF16)
    cwo_b = conv_w_o.astype(BF16)
    xwo_b = xattn_w_o.astype(BF16)
    wout_b = w_out.astype(BF16)
    fup_b = ffn_w_up.astype(BF16)
    fdn_b = ffn_w_down.astype(BF16)
    fg = final_norm_g.reshape(1, D_MODEL)

    for layer in range(depth):
        cs = functools.partial(_const_spec, layer=layer)
        kt, vb = pl.pallas_call(
            _kv_kernel,
            grid=(batch,),
            in_specs=[pl.BlockSpec((None, N_MEM, D_MODEL), lambda b: (b, 0, 0)),
                      pl.BlockSpec((None, 1, D_MODEL), lambda b: (layer, 0, 0)),
                      pl.BlockSpec((None, D_MODEL, 2 * D_XATTN), lambda b: (layer, 0, 0))],
            out_specs=[pl.BlockSpec((None, D_XATTN, XATTN_HEADS * N_MEM), lambda b: (b, 0, 0)),
                       pl.BlockSpec((None, XATTN_HEADS * N_MEM, D_XATTN), lambda b: (b, 0, 0))],
            out_shape=[jax.ShapeDtypeStruct((batch, D_XATTN, XATTN_HEADS * N_MEM), BF16),
                       jax.ShapeDtypeStruct((batch, XATTN_HEADS * N_MEM, D_XATTN), BF16)],
            name=f"xattn_kv_l{layer}",
        )(mem, _row(mem_norm_g), wkv_b)

        mixer_in = [
            (x, tile_spec(ts)),
            (kt, pl.BlockSpec((None, D_XATTN, XATTN_HEADS * N_MEM), lambda b, s: (b, 0, 0))),
            (vb, pl.BlockSpec((None, XATTN_HEADS * N_MEM, D_XATTN), lambda b, s: (b, 0, 0))),
        ]
        for arr in (_row(attn_norm_g), win_p, _row(b_gate), mu_p, _row(rwkv_w0), wup_p,
                    _row(rwkv_a0), aup_p, gup_p, _row(rwkv_k_k), _row(rwkv_k_a),
                    _row(rwkv_r_k.reshape(depth, D_RWKV)), _row(rwkv_gn_w), _row(rwkv_gn_b),
                    wo_b, conv_w_dw, _row(conv_b_dw), _row(conv_ln_g), _row(conv_ln_b), cwo_b,
                    _row(conv_b_o), xwo_b, wout_b):
            mixer_in.append((arr, cs(arr.shape)))
        mixer_in += [(ones_bd, shared2(ones_bd)), (tri, shared2(tri))]
        x = pl.pallas_call(
            _mixer_kernel,
            grid=(batch, seq // ts),
            in_specs=[spec for _, spec in mixer_in],
            out_specs=tile_spec(ts),
            out_shape=jax.ShapeDtypeStruct(x.shape, F32),
            scratch_shapes=[pltpu.VMEM((SUBLANES, C_CONV), F32),
                            pltpu.VMEM((HIST, D_CONV), F32),
                            pltpu.VMEM((N_PAIR, PAIR, PAIR), F32),
                            pltpu.VMEM((HIST + ts, D_CONV), F32),
                            pltpu.VMEM((HIST + ts, D_CONV), F32)],
            compiler_params=params,
            name=f"mixer_l{layer}",
        )(*[arr for arr, _ in mixer_in])

        ffn_in = [(x, tile_spec(ts_ffn))]
        for arr in (_row(ffn_norm_g), fup_b, ffn_w_dw, _row(ffn_b_dw), fdn_b):
            ffn_in.append((arr, cs(arr.shape)))
        ffn_in.append((fg, shared2(fg)))
        x = pl.pallas_call(
            functools.partial(_ffn_kernel, final_norm=(layer == depth - 1)),
            grid=(batch, seq // ts_ffn),
            in_specs=[spec for _, spec in ffn_in],
            out_specs=tile_spec(ts_ffn),
            out_shape=jax.ShapeDtypeStruct(x.shape, F32),
            scratch_shapes=[pltpu.VMEM((SUBLANES, 2 * D_FF), F32)],
            compiler_params=params,
            name=f"ffn_l{layer}",
        )(*[arr for arr, _ in ffn_in])
    return x
```

```python
import functools

import jax
import jax.numpy as jnp
from jax import lax
from jax.experimental import pallas as pl
from jax.experimental.pallas import tpu as pltpu

F32 = jnp.float32
BF16 = jnp.bfloat16

D_MODEL = 1024
N_MEM = 256
RWKV_HEADS = 8
HEAD_DIM = 64
D_RWKV = RWKV_HEADS * HEAD_DIM
DECAY_LORA = 64
ICLR_LORA = 64
GATE_LORA = 160
D_CONV = 256
CONV_WIDTH = 31
XATTN_HEADS = 4
D_XATTN = XATTN_HEADS * HEAD_DIM
N_BRANCH = 3
D_FF = 2816
FFN_CONV_WIDTH = 3
RMS_EPS = 1e-6
LN_EPS = 1e-5
GN_EPS = 64e-5

LANES = 128
SUBLANES = 8
CHUNK = 64
PAIR = 2 * HEAD_DIM
N_PAIR = D_RWKV // PAIR

WD_PAD = LANES
AD_PAD = LANES
GD_PAD = 2 * LANES
C_WD = 3 * D_RWKV
C_AD = C_WD + WD_PAD
C_GD = C_AD + AD_PAD
C_CONV = C_GD + GD_PAD
C_Q = C_CONV + 2 * D_CONV
C_GATE = C_Q + D_XATTN
N_IN_PAD = C_GATE + N_BRANCH * D_MODEL

HIST = 32
FF_BLOCK = 256
MIXER_TILE = 512
FFN_TILE = 1024
FFN_SUB = 256
VMEM_LIMIT_BYTES = 56 * 1024 * 1024


def _dot(a, b):
    return jnp.dot(a, b, preferred_element_type=F32)


def _sigmoid(x):
    return 0.5 * jnp.tanh(0.5 * x) + 0.5


def _split2(x):
    hi = x.astype(BF16)
    return hi, (x - hi.astype(F32)).astype(BF16)


def _cumsum_dot(tri_bf16, x):
    hi, lo = _split2(x)
    return _dot(tri_bf16, hi) + _dot(tri_bf16, lo)


def _head_sum(x, ones_half):
    xb = x.astype(BF16)
    half = ones_half.shape[0]
    return jnp.concatenate([_dot(xb[:, 0:half], ones_half), _dot(xb[:, half:], ones_half)],
                           axis=1)


def _bdot(a, b):
    return lax.dot_general(a, b, (((2,), (1,)), ((0,), (0,))), preferred_element_type=F32)


def _bdot_nt(a, b):
    return lax.dot_general(a, b, (((2,), (2,)), ((0,), (0,))), preferred_element_type=F32)


def _two(x):
    lane = lax.broadcasted_iota(jnp.int32, x.shape, 2)
    first = lane < HEAD_DIM
    return jnp.concatenate([jnp.where(first, x, 0.0), jnp.where(first, 0.0, x)], axis=1)


def _blocks(x, n_chunk):
    return jnp.stack([x[c * CHUNK:(c + 1) * CHUNK, q * PAIR:(q + 1) * PAIR]
                      for c in range(n_chunk) for q in range(N_PAIR)])


def _core_scores(ah, rh, bh, kh, n_chunk):
    a_b, r_b, b_b, k_b = (_blocks(t, n_chunk) for t in (ah, rh, bh, kh))
    t_idx = lax.broadcasted_iota(jnp.int32, (1, CHUNK, PAIR), 1)
    j_idx = lax.broadcasted_iota(jnp.int32, (1, CHUNK, PAIR), 2) & (HEAD_DIM - 1)
    strict = j_idx < t_idx
    incl = j_idx <= t_idx
    sc = _bdot_nt(jnp.concatenate([a_b, r_b], axis=1),
                  jnp.concatenate([_two(b_b), _two(k_b)], axis=1))
    m_ab = jnp.where(strict, sc[:, 0:CHUNK, 0:PAIR], 0.0)
    m_ak = jnp.where(strict, sc[:, 0:CHUNK, PAIR:2 * PAIR], 0.0)
    n_rb = jnp.where(incl, sc[:, CHUNK:2 * CHUNK, 0:PAIR], 0.0)
    n_rk = jnp.where(incl, sc[:, CHUNK:2 * CHUNK, PAIR:2 * PAIR], 0.0)
    return a_b, r_b, m_ab, m_ak, n_rb, n_rk


def _core_inverse(m_ab):
    t_idx = lax.broadcasted_iota(jnp.int32, (1, CHUNK, PAIR), 1)
    j_idx = lax.broadcasted_iota(jnp.int32, (1, CHUNK, PAIR), 2) & (HEAD_DIM - 1)
    t_inv = jnp.where(j_idx == t_idx, 1.0, 0.0) + m_ab
    m_pow = _bdot(m_ab, _two(m_ab))
    for _ in range(4):
        res = _bdot(m_pow, jnp.concatenate([_two(m_pow), _two(t_inv)], axis=2))
        m_pow = res[:, :, 0:PAIR]
        t_inv = t_inv + res[:, :, PAIR:2 * PAIR]
    return t_inv + _bdot(m_pow, _two(t_inv))


def _core_transfer(t_inv, a_b, r_b, m_ak, n_rb, n_rk, v, b_end, k_end, pl_row, n_chunk):
    v_b = _blocks(v, n_chunk)
    nb = v_b.shape[0]
    makv = _bdot(m_ak, _two(v_b))
    res = _bdot(t_inv, jnp.concatenate([_two(a_b), _two(makv)], axis=2))
    w_til = res[:, :, 0:PAIR]
    u_til = res[:, :, PAIR:2 * PAIR]

    bk_end = jnp.concatenate([_blocks(b_end, n_chunk), _blocks(k_end, n_chunk)], axis=1)
    bk_end_t = jnp.stack([bk_end[i].T for i in range(nb)])
    ch = _bdot(bk_end_t, jnp.concatenate(
        [jnp.concatenate([w_til, u_til], axis=2),
         jnp.concatenate([jnp.zeros_like(v_b), v_b], axis=2)], axis=1))
    r_idx = lax.broadcasted_iota(jnp.int32, (1, PAIR, PAIR), 1)
    c_idx = lax.broadcasted_iota(jnp.int32, (1, PAIR, PAIR), 2)
    same_head = (r_idx < HEAD_DIM) == (c_idx < HEAD_DIM)
    c_mat = jnp.where(same_head, ch[:, :, 0:PAIR], 0.0)
    h_mat = jnp.where(same_head, ch[:, :, PAIR:2 * PAIR], 0.0)
    pl_b = jnp.stack([pl_row[c * CHUNK:c * CHUNK + 1, q * PAIR:(q + 1) * PAIR]
                      for c in range(n_chunk) for q in range(N_PAIR)])
    pl_col = jnp.sum(jnp.where(r_idx == c_idx, jnp.broadcast_to(pl_b, (nb, PAIR, PAIR)), 0.0),
                     axis=2, keepdims=True)

    v2 = _two(v_b)
    qy = _bdot(jnp.concatenate([n_rb, n_rk], axis=2), jnp.concatenate(
        [jnp.concatenate([_two(w_til), _two(u_til)], axis=2),
         jnp.concatenate([jnp.zeros_like(v2), v2], axis=2)], axis=1))
    return c_mat, h_mat, pl_col, r_b + qy[:, :, 0:PAIR], qy[:, :, PAIR:2 * PAIR]


def _core_scan(c_mat, h_mat, pl_col, q_mat, y0, state, n_chunk):
    starts = []
    for c in range(n_chunk):
        bs = slice(c * N_PAIR, (c + 1) * N_PAIR)
        starts.append(state)
        state = pl_col[bs] * state + _bdot(c_mat[bs], state) + h_mat[bs]
    y_b = _bdot(q_mat, jnp.concatenate(starts, axis=0)) + y0
    y = jnp.concatenate(
        [jnp.concatenate([y_b[c * N_PAIR + q] for q in range(N_PAIR)], axis=1)
         for c in range(n_chunk)], axis=0)
    return y, state


def _mixer_kernel(x_ref, kt_ref, vb_ref, g_ref, win_ref, bg_ref, mu_ref, w0_ref, wup_ref,
                  a0_ref, aup_ref, gup_ref, kkw_ref, ka_ref, rk_ref, gnw_ref, gnb_ref, wo_ref,
                  cw_ref, cb_ref, lng_ref, lnb_ref, cwo_ref, cbo_ref, xwo_ref, wout_ref,
                  ones_ref, tri_ref, o_ref,
                  prev_ref, hist_ref, state_ref, ubuf_ref, shift_ref):
    ts = x_ref.shape[0]
    n_chunk = ts // CHUNK

    @pl.when(pl.program_id(1) == 0)
    def _():
        prev_ref[...] = jnp.zeros_like(prev_ref)
        hist_ref[...] = jnp.zeros_like(hist_ref)
        state_ref[...] = jnp.zeros_like(state_ref)

    def gate_pre(j):
        lo = C_GATE + j * D_MODEL
        return _dot(hb, win_ref[:, lo:lo + D_MODEL]) + bg_ref[:, j * D_MODEL:(j + 1) * D_MODEL]

    x = x_ref[...]
    ms = jnp.mean(x * x, axis=-1, keepdims=True)
    hb = ((x * lax.rsqrt(ms + RMS_EPS)) * g_ref[...]).astype(BF16)
    row = lax.broadcasted_iota(jnp.int32, (ts, 1), 0)

    p = _dot(hb, win_ref[:, 0:C_CONV])
    pc = _dot(hb, win_ref[:, C_CONV:C_Q])
    qv = _dot(hb, win_ref[:, C_Q:C_GATE])

    shifted = jnp.where(row == 0, prev_ref[0:1, :], pltpu.roll(p, 1, 0))
    prev_ref[0:1, :] = p[ts - 1:ts, :]
    p = p + mu_ref[...] * (shifted - p)
    gate_a = gate_pre(0)
    r = p[:, 0:D_RWKV]
    k = p[:, D_RWKV:2 * D_RWKV]
    v = p[:, 2 * D_RWKV:3 * D_RWKV]
    wd = p[:, C_WD:C_AD]
    ad = p[:, C_AD:C_GD]
    gd = p[:, C_GD:C_CONV]

    z = w0_ref[...] + _dot(jnp.tanh(wd).astype(BF16), wup_ref[...])
    lw = (-jnp.exp(F32(-0.5))) * _sigmoid(z)
    a = _sigmoid(a0_ref[...] + _dot(ad.astype(BF16), aup_ref[...]))
    g_out = _dot(_sigmoid(gd).astype(BF16), gup_ref[...])

    ones_bd = ones_ref[...]
    kk = k * kkw_ref[...]
    ssq = _head_sum(kk * kk, ones_bd)
    kk = kk / jnp.maximum(jnp.sqrt(ssq), 1e-12)
    k = k * (1.0 + (a - 1.0) * ka_ref[...])
    gate_b = gate_pre(1)

    cum = _cumsum_dot(tri_ref[...], lw)
    cum_end = jnp.concatenate(
        [jnp.broadcast_to(cum[(c + 1) * CHUNK - 1:(c + 1) * CHUNK, :], (CHUNK, D_RWKV))
         for c in range(n_chunk)], axis=0)
    e_inv = jnp.exp(-cum)
    e_end = jnp.exp(cum_end - cum)
    kka = kk * a
    rh_all = r * jnp.exp(cum)
    ah_all = -kk * jnp.exp(cum - lw)
    bh_all = kka * e_inv
    kh_all = k * e_inv
    pl_all = jnp.exp(cum_end)

    gate_c = gate_pre(2)
    sc = _dot(qv.astype(BF16), kt_ref[...])
    a_b, r_b, m_ab, m_ak, n_rb, n_rk = _core_scores(ah_all, rh_all, bh_all, kh_all, n_chunk)

    u = pc[:, 0:D_CONV] * _sigmoid(pc[:, D_CONV:2 * D_CONV])
    ubuf_ref[0:HIST, :] = hist_ref[...]
    ubuf_ref[HIST:HIST + ts, :] = u
    hist_ref[...] = u[ts - HIST:ts, :]
    acc = jnp.broadcast_to(cb_ref[...], (ts, D_CONV))
    first = HIST - (CONV_WIDTH - 1)
    for phase in range(SUBLANES):
        taps = range(phase, CONV_WIDTH, SUBLANES)
        span = ts + SUBLANES * (len(taps) - 1)
        shift_ref[0:span, :] = ubuf_ref[first + phase:first + phase + span, :]
        for i, tap in enumerate(taps):
            acc = acc + cw_ref[tap:tap + 1, :] * shift_ref[SUBLANES * i:SUBLANES * i + ts, :]

    t_inv = _core_inverse(m_ab)

    mu_ln = jnp.mean(acc, axis=-1, keepdims=True)
    dev = acc - mu_ln
    var = jnp.mean(dev * dev, axis=-1, keepdims=True)
    u = dev * lax.rsqrt(var + LN_EPS) * lng_ref[...] + lnb_ref[...]
    u = u * _sigmoid(u)
    y_b = _dot(u.astype(BF16), cwo_ref[...]) + cbo_ref[...]

    c_mat, h_mat, pl_col, q_mat, y0 = _core_transfer(
        t_inv, a_b, r_b, m_ak, n_rb, n_rk, v, kka * e_end, k * e_end, pl_all, n_chunk)

    probs = []
    for h in range(XATTN_HEADS):
        s_h = sc[:, h * N_MEM:(h + 1) * N_MEM]
        e_h = jnp.exp(s_h - jnp.max(s_h, axis=-1, keepdims=True))
        probs.append(e_h / jnp.sum(e_h, axis=-1, keepdims=True))
    pr = jnp.concatenate(probs, axis=1).astype(BF16)

    y, new_state = _core_scan(c_mat, h_mat, pl_col, q_mat, y0, state_ref[...], n_chunk)
    state_ref[...] = new_state

    o_c = _dot(pr, vb_ref[...])
    y_c = _dot(o_c.astype(BF16), xwo_ref[...])
    merged = _sigmoid(gate_b) * y_b + _sigmoid(gate_c) * y_c

    inv_n = F32(1.0 / HEAD_DIM)
    mean = _head_sum(y, ones_bd) * inv_n
    dev = y - mean
    var = _head_sum(dev * dev, ones_bd) * inv_n
    y = dev * lax.rsqrt(var + GN_EPS) * gnw_ref[...] + gnb_ref[...]
    bonus = _head_sum(r * k * rk_ref[...], ones_bd) * v
    y_a = _dot(((y + bonus) * g_out).astype(BF16), wo_ref[...])
    merged = merged + _sigmoid(gate_a) * y_a

    o_ref[...] = x + _dot(merged.astype(BF16), wout_ref[...])


def _ffn_kernel(x_ref, g_ref, wup_ref, wdw_ref, bdw_ref, wdn_ref, fg_ref, o_ref, carry_ref,
                *, final_norm):
    @pl.when(pl.program_id(1) == 0)
    def _():
        carry_ref[...] = jnp.zeros_like(carry_ref)

    ts = min(FFN_SUB, x_ref.shape[0])
    row = lax.broadcasted_iota(jnp.int32, (ts, 1), 0)
    n_blk = D_FF // FF_BLOCK

    for r0 in range(0, x_ref.shape[0], ts):
        x = x_ref[r0:r0 + ts, :]
        ms = jnp.mean(x * x, axis=-1, keepdims=True)
        hb = ((x * lax.rsqrt(ms + RMS_EPS)) * g_ref[...]).astype(BF16)

        def up_block(col, hb=hb):
            return _dot(hb, wup_ref[:, col:col + FF_BLOCK])

        def conv_block(zb, col):
            cs = slice(col, col + FF_BLOCK)
            tail = carry_ref[:, cs]
            carry_ref[:, cs] = zb[ts - SUBLANES:ts, :]
            z1 = jnp.where(row == 0, tail[SUBLANES - 1:SUBLANES, :], pltpu.roll(zb, 1, 0))
            z2 = jnp.where(row == 0, tail[SUBLANES - 2:SUBLANES - 1, :],
                           jnp.where(row == 1, tail[SUBLANES - 1:SUBLANES, :],
                                     pltpu.roll(zb, 2, 0)))
            return (wdw_ref[0:1, cs] * z2 + wdw_ref[1:2, cs] * z1 + wdw_ref[2:3, cs] * zb
                    + bdw_ref[:, cs])

        acc = jnp.zeros((ts, D_MODEL), F32)
        z_next = (up_block(0), up_block(D_FF))
        act_prev = None
        for j in range(n_blk):
            zu, zg = z_next
            if j + 1 < n_blk:
                z_next = (up_block((j + 1) * FF_BLOCK), up_block(D_FF + (j + 1) * FF_BLOCK))
            if act_prev is not None:
                acc = acc + _dot(act_prev, wdn_ref[(j - 1) * FF_BLOCK:j * FF_BLOCK, :])
            u = conv_block(zu, j * FF_BLOCK)
            gt = conv_block(zg, D_FF + j * FF_BLOCK)
            half = 0.5 * gt
            act_prev = ((half + half * jnp.tanh(half)) * u).astype(BF16)
        acc = acc + _dot(act_prev, wdn_ref[(n_blk - 1) * FF_BLOCK:n_blk * FF_BLOCK, :])
        out = x + acc
        if final_norm:
            ms = jnp.mean(out * out, axis=-1, keepdims=True)
            out = out * lax.rsqrt(ms + RMS_EPS) * fg_ref[...]
        o_ref[r0:r0 + ts, :] = out


def _kv_kernel(mem_ref, g_ref, wkv_ref, kt_ref, vb_ref):
    m = mem_ref[...]
    ms = jnp.mean(m * m, axis=-1, keepdims=True)
    mb = ((m * lax.rsqrt(ms + RMS_EPS)) * g_ref[...]).astype(BF16)
    kv = _dot(mb, wkv_ref[...])
    k_t = kv[:, 0:D_XATTN].T * F32(HEAD_DIM ** -0.5)
    vv = kv[:, D_XATTN:2 * D_XATTN]
    r_idx = lax.broadcasted_iota(jnp.int32, (D_XATTN, XATTN_HEADS * N_MEM), 0)
    c_idx = lax.broadcasted_iota(jnp.int32, (D_XATTN, XATTN_HEADS * N_MEM), 1)
    kt_ref[...] = jnp.where(r_idx // HEAD_DIM == c_idx // N_MEM,
                            jnp.concatenate([k_t] * XATTN_HEADS, axis=1), 0.0).astype(BF16)
    r_idx = lax.broadcasted_iota(jnp.int32, (XATTN_HEADS * N_MEM, D_XATTN), 0)
    c_idx = lax.broadcasted_iota(jnp.int32, (XATTN_HEADS * N_MEM, D_XATTN), 1)
    vb_ref[...] = jnp.where(r_idx // N_MEM == c_idx // HEAD_DIM,
                            jnp.concatenate([vv] * XATTN_HEADS, axis=0), 0.0).astype(BF16)


def _const_spec(shape, layer):
    nd = len(shape)
    return pl.BlockSpec((None,) + tuple(shape[1:]), lambda b, s: (layer,) + (0,) * (nd - 1),
                        pipeline_mode=pl.Buffered(1))


def _pad_axis(x, axis, new):
    pad = [(0, 0)] * x.ndim
    pad[axis] = (0, new - x.shape[axis])
    return jnp.pad(x, pad)


def _row(x):
    return x.reshape(x.shape[0], 1, -1)


def kernel(x, mem, attn_norm_g, w_in, b_gate, rwkv_mu, rwkv_w0, rwkv_w_up, rwkv_a0,
           rwkv_a_up, rwkv_g_up, rwkv_k_k, rwkv_k_a, rwkv_r_k, rwkv_gn_w, rwkv_gn_b,
           rwkv_w_o, conv_w_dw, conv_b_dw, conv_ln_g, conv_ln_b, conv_w_o, conv_b_o,
           mem_norm_g, xattn_w_kv, xattn_w_o, w_out, ffn_norm_g, ffn_w_up, ffn_w_dw,
           ffn_b_dw, ffn_w_down, final_norm_g):
    batch, seq, _ = x.shape
    depth = w_in.shape[0]
    ts = min(MIXER_TILE, seq)
    ts_ffn = min(FFN_TILE, seq)
    assert seq % ts == 0 and ts % CHUNK == 0 and ts >= HIST
    assert seq % ts_ffn == 0 and ts_ffn % min(FFN_SUB, ts_ffn) == 0
    c1 = 3 * D_RWKV
    c_rw = c1 + DECAY_LORA + ICLR_LORA + GATE_LORA

    def pad_cols(w):
        return jnp.concatenate([
            w[..., 0:c1],
            _pad_axis(w[..., c1:c1 + DECAY_LORA], -1, WD_PAD),
            _pad_axis(w[..., c1 + DECAY_LORA:c1 + DECAY_LORA + ICLR_LORA], -1, AD_PAD),
            _pad_axis(w[..., c1 + DECAY_LORA + ICLR_LORA:c_rw], -1, GD_PAD),
            w[..., c_rw:]], axis=-1)

    win_p = pad_cols(w_in).astype(BF16)
    mu_p = _row(pad_cols(rwkv_mu))
    wup_p = _pad_axis(rwkv_w_up, 1, WD_PAD).astype(BF16)
    aup_p = _pad_axis(rwkv_a_up, 1, AD_PAD).astype(BF16)
    gup_p = _pad_axis(rwkv_g_up, 1, GD_PAD).astype(BF16)
    head_of = jnp.arange(D_RWKV // 2) // HEAD_DIM
    ones_bd = (head_of[:, None] == head_of[None, :]).astype(BF16)
    tok = jnp.arange(ts)
    tri = ((tok[:, None] // CHUNK == tok[None, :] // CHUNK)
           & (tok[None, :] <= tok[:, None])).astype(BF16)

    params = pltpu.CompilerParams(dimension_semantics=("arbitrary", "arbitrary"),
                                  vmem_limit_bytes=VMEM_LIMIT_BYTES)
    tile_spec = lambda rows: pl.BlockSpec((None, rows, D_MODEL), lambda b, s: (b, s, 0))
    shared2 = lambda a: pl.BlockSpec(a.shape, lambda b, s: (0, 0), pipeline_mode=pl.Buffered(1))

    wkv_b = xattn_w_kv.astype(BF16)
    wo_b = rwkv_w_o.astype(BF16)
    cwo_b = conv_w_o.astype(BF16)
    xwo_b = xattn_w_o.astype(BF16)
    wout_b = w_out.astype(BF16)
    fup_b = ffn_w_up.astype(BF16)
    fdn_b = ffn_w_down.astype(BF16)
    fg = final_norm_g.reshape(1, D_MODEL)

    for layer in range(depth):
        cs = functools.partial(_const_spec, layer=layer)
        kt, vb = pl.pallas_call(
            _kv_kernel,
            grid=(batch,),
            in_specs=[pl.BlockSpec((None, N_MEM, D_MODEL), lambda b: (b, 0, 0)),
                      pl.BlockSpec((None, 1, D_MODEL), lambda b: (layer, 0, 0)),
                      pl.BlockSpec((None, D_MODEL, 2 * D_XATTN), lambda b: (layer, 0, 0))],
            out_specs=[pl.BlockSpec((None, D_XATTN, XATTN_HEADS * N_MEM), lambda b: (b, 0, 0)),
                       pl.BlockSpec((None, XATTN_HEADS * N_MEM, D_XATTN), lambda b: (b, 0, 0))],
            out_shape=[jax.ShapeDtypeStruct((batch, D_XATTN, XATTN_HEADS * N_MEM), BF16),
                       jax.ShapeDtypeStruct((batch, XATTN_HEADS * N_MEM, D_XATTN), BF16)],
            name=f"xattn_kv_l{layer}",
        )(mem, _row(mem_norm_g), wkv_b)

        mixer_in = [
            (x, tile_spec(ts)),
            (kt, pl.BlockSpec((None, D_XATTN, XATTN_HEADS * N_MEM), lambda b, s: (b, 0, 0))),
            (vb, pl.BlockSpec((None, XATTN_HEADS * N_MEM, D_XATTN), lambda b, s: (b, 0, 0))),
        ]
        for arr in (_row(attn_norm_g), win_p, _row(b_gate), mu_p, _row(rwkv_w0), wup_p,
                    _row(rwkv_a0), aup_p, gup_p, _row(rwkv_k_k), _row(rwkv_k_a),
                    _row(rwkv_r_k.reshape(depth, D_RWKV)), _row(rwkv_gn_w), _row(rwkv_gn_b),
                    wo_b, conv_w_dw, _row(conv_b_dw), _row(conv_ln_g), _row(conv_ln_b), cwo_b,
                    _row(conv_b_o), xwo_b, wout_b):
            mixer_in.append((arr, cs(arr.shape)))
        mixer_in += [(ones_bd, shared2(ones_bd)), (tri, shared2(tri))]
        x = pl.pallas_call(
            _mixer_kernel,
            grid=(batch, seq // ts),
            in_specs=[spec for _, spec in mixer_in],
            out_specs=tile_spec(ts),
            out_shape=jax.ShapeDtypeStruct(x.shape, F32),
            scratch_shapes=[pltpu.VMEM((SUBLANES, C_CONV), F32),
                            pltpu.VMEM((HIST, D_CONV), F32),
                            pltpu.VMEM((N_PAIR, PAIR, PAIR), F32),
                            pltpu.VMEM((HIST + ts, D_CONV), F32),
                            pltpu.VMEM((HIST + ts, D_CONV), F32)],
            compiler_params=params,
            name=f"mixer_l{layer}",
        )(*[arr for arr, _ in mixer_in])

        ffn_in = [(x, tile_spec(ts_ffn))]
        for arr in (_row(ffn_norm_g), fup_b, ffn_w_dw, _row(ffn_b_dw), fdn_b):
            ffn_in.append((arr, cs(arr.shape)))
        ffn_in.append((fg, shared2(fg)))
        x = pl.pallas_call(
            functools.partial(_ffn_kernel, final_norm=(layer == depth - 1)),
            grid=(batch, seq // ts_ffn),
            in_specs=[spec for _, spec in ffn_in],
            out_specs=tile_spec(ts_ffn),
            out_shape=jax.ShapeDtypeStruct(x.shape, F32),
            scratch_shapes=[pltpu.VMEM((SUBLANES, 2 * D_FF), F32)],
            compiler_params=params,
            name=f"ffn_l{layer}",
        )(*[arr for arr, _ in ffn_in])
    return x
```

```python
import functools

import jax
import jax.numpy as jnp
from jax import lax
from jax.experimental import pallas as pl
from jax.experimental.pallas import tpu as pltpu

F32 = jnp.float32
BF16 = jnp.bfloat16

D_MODEL = 1024
N_MEM = 256
RWKV_HEADS = 8
HEAD_DIM = 64
D_RWKV = RWKV_HEADS * HEAD_DIM
DECAY_LORA = 64
ICLR_LORA = 64
GATE_LORA = 160
D_CONV = 256
CONV_WIDTH = 31
XATTN_HEADS = 4
D_XATTN = XATTN_HEADS * HEAD_DIM
N_BRANCH = 3
D_FF = 2816
FFN_CONV_WIDTH = 3
RMS_EPS = 1e-6
LN_EPS = 1e-5
GN_EPS = 64e-5

LANES = 128
SUBLANES = 8
CHUNK = 64
PAIR = 2 * HEAD_DIM
N_PAIR = D_RWKV // PAIR

WD_PAD = LANES
AD_PAD = LANES
GD_PAD = 2 * LANES
C_WD = 3 * D_RWKV
C_AD = C_WD + WD_PAD
C_GD = C_AD + AD_PAD
C_CONV = C_GD + GD_PAD
C_Q = C_CONV + 2 * D_CONV
C_GATE = C_Q + D_XATTN
N_IN_PAD = C_GATE + N_BRANCH * D_MODEL

HIST = 32
FF_BLOCK = 256
MIXER_TILE = 512
FFN_TILE = 512
VMEM_LIMIT_BYTES = 56 * 1024 * 1024


def _dot(a, b):
    return jnp.dot(a, b, preferred_element_type=F32)


def _sigmoid(x):
    return 0.5 * jnp.tanh(0.5 * x) + 0.5


def _split2(x):
    hi = x.astype(BF16)
    return hi, (x - hi.astype(F32)).astype(BF16)


def _cumsum_dot(tri_bf16, x):
    hi, lo = _split2(x)
    return _dot(tri_bf16, hi) + _dot(tri_bf16, lo)


def _head_sum(x, ones_half):
    xb = x.astype(BF16)
    half = ones_half.shape[0]
    return jnp.concatenate([_dot(xb[:, 0:half], ones_half), _dot(xb[:, half:], ones_half)],
                           axis=1)


def _bdot(a, b):
    return lax.dot_general(a, b, (((2,), (1,)), ((0,), (0,))), preferred_element_type=F32)


def _bdot_nt(a, b):
    return lax.dot_general(a, b, (((2,), (2,)), ((0,), (0,))), preferred_element_type=F32)


def _two(x):
    lane = lax.broadcasted_iota(jnp.int32, x.shape, 2)
    first = lane < HEAD_DIM
    return jnp.concatenate([jnp.where(first, x, 0.0), jnp.where(first, 0.0, x)], axis=1)


def _blocks(x, n_chunk):
    return jnp.stack([x[c * CHUNK:(c + 1) * CHUNK, q * PAIR:(q + 1) * PAIR]
                      for c in range(n_chunk) for q in range(N_PAIR)])


def _core_scores(ah, rh, bh, kh, n_chunk):
    a_b, r_b, b_b, k_b = (_blocks(t, n_chunk) for t in (ah, rh, bh, kh))
    t_idx = lax.broadcasted_iota(jnp.int32, (1, CHUNK, PAIR), 1)
    j_idx = lax.broadcasted_iota(jnp.int32, (1, CHUNK, PAIR), 2) & (HEAD_DIM - 1)
    strict = j_idx < t_idx
    incl = j_idx <= t_idx
    sc = _bdot_nt(jnp.concatenate([a_b, r_b], axis=1),
                  jnp.concatenate([_two(b_b), _two(k_b)], axis=1))
    m_ab = jnp.where(strict, sc[:, 0:CHUNK, 0:PAIR], 0.0)
    m_ak = jnp.where(strict, sc[:, 0:CHUNK, PAIR:2 * PAIR], 0.0)
    n_rb = jnp.where(incl, sc[:, CHUNK:2 * CHUNK, 0:PAIR], 0.0)
    n_rk = jnp.where(incl, sc[:, CHUNK:2 * CHUNK, PAIR:2 * PAIR], 0.0)
    return a_b, r_b, m_ab, m_ak, n_rb, n_rk


def _core_inverse(m_ab):
    t_idx = lax.broadcasted_iota(jnp.int32, (1, CHUNK, PAIR), 1)
    j_idx = lax.broadcasted_iota(jnp.int32, (1, CHUNK, PAIR), 2) & (HEAD_DIM - 1)
    t_inv = jnp.where(j_idx == t_idx, 1.0, 0.0) + m_ab
    m_pow = _bdot(m_ab, _two(m_ab))
    for _ in range(4):
        res = _bdot(m_pow, jnp.concatenate([_two(m_pow), _two(t_inv)], axis=2))
        m_pow = res[:, :, 0:PAIR]
        t_inv = t_inv + res[:, :, PAIR:2 * PAIR]
    return t_inv + _bdot(m_pow, _two(t_inv))


def _core_transfer(t_inv, a_b, r_b, m_ak, n_rb, n_rk, v, b_end, k_end, pl_row, n_chunk):
    v_b = _blocks(v, n_chunk)
    nb = v_b.shape[0]
    makv = _bdot(m_ak, _two(v_b))
    res = _bdot(t_inv, jnp.concatenate([_two(a_b), _two(makv)], axis=2))
    w_til = res[:, :, 0:PAIR]
    u_til = res[:, :, PAIR:2 * PAIR]

    bk_end = jnp.concatenate([_blocks(b_end, n_chunk), _blocks(k_end, n_chunk)], axis=1)
    bk_end_t = jnp.stack([bk_end[i].T for i in range(nb)])
    ch = _bdot(bk_end_t, jnp.concatenate(
        [jnp.concatenate([w_til, u_til], axis=2),
         jnp.concatenate([jnp.zeros_like(v_b), v_b], axis=2)], axis=1))
    r_idx = lax.broadcasted_iota(jnp.int32, (1, PAIR, PAIR), 1)
    c_idx = lax.broadcasted_iota(jnp.int32, (1, PAIR, PAIR), 2)
    same_head = (r_idx < HEAD_DIM) == (c_idx < HEAD_DIM)
    c_mat = jnp.where(same_head, ch[:, :, 0:PAIR], 0.0)
    h_mat = jnp.where(same_head, ch[:, :, PAIR:2 * PAIR], 0.0)
    pl_b = jnp.stack([pl_row[c * CHUNK:c * CHUNK + 1, q * PAIR:(q + 1) * PAIR]
                      for c in range(n_chunk) for q in range(N_PAIR)])
    pl_col = jnp.sum(jnp.where(r_idx == c_idx, jnp.broadcast_to(pl_b, (nb, PAIR, PAIR)), 0.0),
                     axis=2, keepdims=True)

    v2 = _two(v_b)
    qy = _bdot(jnp.concatenate([n_rb, n_rk], axis=2), jnp.concatenate(
        [jnp.concatenate([_two(w_til), _two(u_til)], axis=2),
         jnp.concatenate([jnp.zeros_like(v2), v2], axis=2)], axis=1))
    return c_mat, h_mat, pl_col, r_b + qy[:, :, 0:PAIR], qy[:, :, PAIR:2 * PAIR]


def _core_scan(c_mat, h_mat, pl_col, q_mat, y0, state, n_chunk):
    starts = []
    for c in range(n_chunk):
        bs = slice(c * N_PAIR, (c + 1) * N_PAIR)
        starts.append(state)
        state = pl_col[bs] * state + _bdot(c_mat[bs], state) + h_mat[bs]
    y_b = _bdot(q_mat, jnp.concatenate(starts, axis=0)) + y0
    y = jnp.concatenate(
        [jnp.concatenate([y_b[c * N_PAIR + q] for q in range(N_PAIR)], axis=1)
         for c in range(n_chunk)], axis=0)
    return y, state


def _mixer_kernel(x_ref, kt_ref, vb_ref, g_ref, win_ref, bg_ref, mu_ref, w0_ref, wup_ref,
                  a0_ref, aup_ref, gup_ref, kkw_ref, ka_ref, rk_ref, gnw_ref, gnb_ref, wo_ref,
                  cw_ref, cb_ref, lng_ref, lnb_ref, cwo_ref, cbo_ref, xwo_ref, wout_ref,
                  ones_ref, tri_ref, o_ref,
                  prev_ref, hist_ref, state_ref, ubuf_ref, shift_ref):
    ts = x_ref.shape[0]
    n_chunk = ts // CHUNK

    @pl.when(pl.program_id(1) == 0)
    def _():
        prev_ref[...] = jnp.zeros_like(prev_ref)
        hist_ref[...] = jnp.zeros_like(hist_ref)
        state_ref[...] = jnp.zeros_like(state_ref)

    def gate_pre(j):
        lo = C_GATE + j * D_MODEL
        return _dot(hb, win_ref[:, lo:lo + D_MODEL]) + bg_ref[:, j * D_MODEL:(j + 1) * D_MODEL]

    x = x_ref[...]
    ms = jnp.mean(x * x, axis=-1, keepdims=True)
    hb = ((x * lax.rsqrt(ms + RMS_EPS)) * g_ref[...]).astype(BF16)
    row = lax.broadcasted_iota(jnp.int32, (ts, 1), 0)

    p = _dot(hb, win_ref[:, 0:C_CONV])
    pc = _dot(hb, win_ref[:, C_CONV:C_Q])
    qv = _dot(hb, win_ref[:, C_Q:C_GATE])

    shifted = jnp.where(row == 0, prev_ref[0:1, :], pltpu.roll(p, 1, 0))
    prev_ref[0:1, :] = p[ts - 1:ts, :]
    p = p + mu_ref[...] * (shifted - p)
    gate_a = gate_pre(0)
    r = p[:, 0:D_RWKV]
    k = p[:, D_RWKV:2 * D_RWKV]
    v = p[:, 2 * D_RWKV:3 * D_RWKV]
    wd = p[:, C_WD:C_AD]
    ad = p[:, C_AD:C_GD]
    gd = p[:, C_GD:C_CONV]

    z = w0_ref[...] + _dot(jnp.tanh(wd).astype(BF16), wup_ref[...])
    lw = (-jnp.exp(F32(-0.5))) * _sigmoid(z)
    a = _sigmoid(a0_ref[...] + _dot(ad.astype(BF16), aup_ref[...]))
    g_out = _dot(_sigmoid(gd).astype(BF16), gup_ref[...])

    ones_bd = ones_ref[...]
    kk = k * kkw_ref[...]
    ssq = _head_sum(kk * kk, ones_bd)
    kk = kk / jnp.maximum(jnp.sqrt(ssq), 1e-12)
    k = k * (1.0 + (a - 1.0) * ka_ref[...])
    gate_b = gate_pre(1)

    cum = _cumsum_dot(tri_ref[...], lw)
    cum_end = jnp.concatenate(
        [jnp.broadcast_to(cum[(c + 1) * CHUNK - 1:(c + 1) * CHUNK, :], (CHUNK, D_RWKV))
         for c in range(n_chunk)], axis=0)
    e_inv = jnp.exp(-cum)
    e_end = jnp.exp(cum_end - cum)
    kka = kk * a
    rh_all = r * jnp.exp(cum)
    ah_all = -kk * jnp.exp(cum - lw)
    bh_all = kka * e_inv
    kh_all = k * e_inv
    pl_all = jnp.exp(cum_end)

    gate_c = gate_pre(2)
    sc = _dot(qv.astype(BF16), kt_ref[...])
    a_b, r_b, m_ab, m_ak, n_rb, n_rk = _core_scores(ah_all, rh_all, bh_all, kh_all, n_chunk)

    u = pc[:, 0:D_CONV] * _sigmoid(pc[:, D_CONV:2 * D_CONV])
    ubuf_ref[0:HIST, :] = hist_ref[...]
    ubuf_ref[HIST:HIST + ts, :] = u
    hist_ref[...] = u[ts - HIST:ts, :]
    acc = jnp.broadcast_to(cb_ref[...], (ts, D_CONV))
    first = HIST - (CONV_WIDTH - 1)
    for phase in range(SUBLANES):
        taps = range(phase, CONV_WIDTH, SUBLANES)
        span = ts + SUBLANES * (len(taps) - 1)
        shift_ref[0:span, :] = ubuf_ref[first + phase:first + phase + span, :]
        for i, tap in enumerate(taps):
            acc = acc + cw_ref[tap:tap + 1, :] * shift_ref[SUBLANES * i:SUBLANES * i + ts, :]

    t_inv = _core_inverse(m_ab)

    mu_ln = jnp.mean(acc, axis=-1, keepdims=True)
    dev = acc - mu_ln
    var = jnp.mean(dev * dev, axis=-1, keepdims=True)
    u = dev * lax.rsqrt(var + LN_EPS) * lng_ref[...] + lnb_ref[...]
    u = u * _sigmoid(u)
    y_b = _dot(u.astype(BF16), cwo_ref[...]) + cbo_ref[...]

    c_mat, h_mat, pl_col, q_mat, y0 = _core_transfer(
        t_inv, a_b, r_b, m_ak, n_rb, n_rk, v, kka * e_end, k * e_end, pl_all, n_chunk)

    probs = []
    for h in range(XATTN_HEADS):
        s_h = sc[:, h * N_MEM:(h + 1) * N_MEM]
        e_h = jnp.exp(s_h - jnp.max(s_h, axis=-1, keepdims=True))
        probs.append(e_h / jnp.sum(e_h, axis=-1, keepdims=True))
    pr = jnp.concatenate(probs, axis=1).astype(BF16)

    y, new_state = _core_scan(c_mat, h_mat, pl_col, q_mat, y0, state_ref[...], n_chunk)
    state_ref[...] = new_state

    o_c = _dot(pr, vb_ref[...])
    y_c = _dot(o_c.astype(BF16), xwo_ref[...])
    merged = _sigmoid(gate_b) * y_b + _sigmoid(gate_c) * y_c

    inv_n = F32(1.0 / HEAD_DIM)
    mean = _head_sum(y, ones_bd) * inv_n
    dev = y - mean
    var = _head_sum(dev * dev, ones_bd) * inv_n
    y = dev * lax.rsqrt(var + GN_EPS) * gnw_ref[...] + gnb_ref[...]
    bonus = _head_sum(r * k * rk_ref[...], ones_bd) * v
    y_a = _dot(((y + bonus) * g_out).astype(BF16), wo_ref[...])
    merged = merged + _sigmoid(gate_a) * y_a

    o_ref[...] = x + _dot(merged.astype(BF16), wout_ref[...])


def _ffn_kernel(x_ref, g_ref, wup_ref, wdw_ref, bdw_ref, wdn_ref, fg_ref, o_ref, carry_ref,
                *, final_norm):
    ts = x_ref.shape[0]

    @pl.when(pl.program_id(1) == 0)
    def _():
        carry_ref[...] = jnp.zeros_like(carry_ref)

    x = x_ref[...]
    ms = jnp.mean(x * x, axis=-1, keepdims=True)
    hb = ((x * lax.rsqrt(ms + RMS_EPS)) * g_ref[...]).astype(BF16)
    row = lax.broadcasted_iota(jnp.int32, (ts, 1), 0)

    def up_block(col):
        return _dot(hb, wup_ref[:, col:col + FF_BLOCK])

    def conv_block(zb, col):
        cs = slice(col, col + FF_BLOCK)
        tail = carry_ref[:, cs]
        carry_ref[:, cs] = zb[ts - SUBLANES:ts, :]
        z1 = jnp.where(row == 0, tail[SUBLANES - 1:SUBLANES, :], pltpu.roll(zb, 1, 0))
        z2 = jnp.where(row == 0, tail[SUBLANES - 2:SUBLANES - 1, :],
                       jnp.where(row == 1, tail[SUBLANES - 1:SUBLANES, :], pltpu.roll(zb, 2, 0)))
        return (wdw_ref[0:1, cs] * z2 + wdw_ref[1:2, cs] * z1 + wdw_ref[2:3, cs] * zb
                + bdw_ref[:, cs])

    n_blk = D_FF // FF_BLOCK
    acc = jnp.zeros((ts, D_MODEL), F32)
    z_next = (up_block(0), up_block(D_FF))
    act_prev = None
    for j in range(n_blk):
        zu, zg = z_next
        if j + 1 < n_blk:
            z_next = (up_block((j + 1) * FF_BLOCK), up_block(D_FF + (j + 1) * FF_BLOCK))
        if act_prev is not None:
            acc = acc + _dot(act_prev, wdn_ref[(j - 1) * FF_BLOCK:j * FF_BLOCK, :])
        u = conv_block(zu, j * FF_BLOCK)
        gt = conv_block(zg, D_FF + j * FF_BLOCK)
        half = 0.5 * gt
        act_prev = ((half + half * jnp.tanh(half)) * u).astype(BF16)
    acc = acc + _dot(act_prev, wdn_ref[(n_blk - 1) * FF_BLOCK:n_blk * FF_BLOCK, :])
    out = x + acc
    if final_norm:
        ms = jnp.mean(out * out, axis=-1, keepdims=True)
        out = out * lax.rsqrt(ms + RMS_EPS) * fg_ref[...]
    o_ref[...] = out


def _kv_kernel(mem_ref, g_ref, wkv_ref, kt_ref, vb_ref):
    m = mem_ref[...]
    ms = jnp.mean(m * m, axis=-1, keepdims=True)
    mb = ((m * lax.rsqrt(ms + RMS_EPS)) * g_ref[...]).astype(BF16)
    kv = _dot(mb, wkv_ref[...])
    k_t = kv[:, 0:D_XATTN].T * F32(HEAD_DIM ** -0.5)
    vv = kv[:, D_XATTN:2 * D_XATTN]
    r_idx = lax.broadcasted_iota(jnp.int32, (D_XATTN, XATTN_HEADS * N_MEM), 0)
    c_idx = lax.broadcasted_iota(jnp.int32, (D_XATTN, XATTN_HEADS * N_MEM), 1)
    kt_ref[...] = jnp.where(r_idx // HEAD_DIM == c_idx // N_MEM,
                            jnp.concatenate([k_t] * XATTN_HEADS, axis=1), 0.0).astype(BF16)
    r_idx = lax.broadcasted_iota(jnp.int32, (XATTN_HEADS * N_MEM, D_XATTN), 0)
    c_idx = lax.broadcasted_iota(jnp.int32, (XATTN_HEADS * N_MEM, D_XATTN), 1)
    vb_ref[...] = jnp.where(r_idx // N_MEM == c_idx // HEAD_DIM,
                            jnp.concatenate([vv] * XATTN_HEADS, axis=0), 0.0).astype(BF16)


def _const_spec(shape, layer):
    nd = len(shape)
    return pl.BlockSpec((None,) + tuple(shape[1:]), lambda b, s: (layer,) + (0,) * (nd - 1),
                        pipeline_mode=pl.Buffered(1))


def _pad_axis(x, axis, new):
    pad = [(0, 0)] * x.ndim
    pad[axis] = (0, new - x.shape[axis])
    return jnp.pad(x, pad)


def _row(x):
    return x.reshape(x.shape[0], 1, -1)


def kernel(x, mem, attn_norm_g, w_in, b_gate, rwkv_mu, rwkv_w0, rwkv_w_up, rwkv_a0,
           rwkv_a_up, rwkv_g_up, rwkv_k_k, rwkv_k_a, rwkv_r_k, rwkv_gn_w, rwkv_gn_b,
           rwkv_w_o, conv_w_dw, conv_b_dw, conv_ln_g, conv_ln_b, conv_w_o, conv_b_o,
           mem_norm_g, xattn_w_kv, xattn_w_o, w_out, ffn_norm_g, ffn_w_up, ffn_w_dw,
           ffn_b_dw, ffn_w_down, final_norm_g):
    batch, seq, _ = x.shape
    depth = w_in.shape[0]
    ts = min(MIXER_TILE, seq)
    ts_ffn = min(FFN_TILE, seq)
    assert seq % ts == 0 and ts % CHUNK == 0 and ts >= HIST
    assert seq % ts_ffn == 0 and ts_ffn % SUBLANES == 0
    c1 = 3 * D_RWKV
    c_rw = c1 + DECAY_LORA + ICLR_LORA + GATE_LORA

    def pad_cols(w):
        return jnp.concatenate([
            w[..., 0:c1],
            _pad_axis(w[..., c1:c1 + DECAY_LORA], -1, WD_PAD),
            _pad_axis(w[..., c1 + DECAY_LORA:c1 + DECAY_LORA + ICLR_LORA], -1, AD_PAD),
            _pad_axis(w[..., c1 + DECAY_LORA + ICLR_LORA:c_rw], -1, GD_PAD),
            w[..., c_rw:]], axis=-1)

    win_p = pad_cols(w_in).astype(BF16)
    mu_p = _row(pad_cols(rwkv_mu))
    wup_p = _pad_axis(rwkv_w_up, 1, WD_PAD).astype(BF16)
    aup_p = _pad_axis(rwkv_a_up, 1, AD_PAD).astype(BF16)
    gup_p = _pad_axis(rwkv_g_up, 1, GD_PAD).astype(BF16)
    head_of = jnp.arange(D_RWKV // 2) // HEAD_DIM
    ones_bd = (head_of[:, None] == head_of[None, :]).astype(BF16)
    tok = jnp.arange(ts)
    tri = ((tok[:, None] // CHUNK == tok[None, :] // CHUNK)
           & (tok[None, :] <= tok[:, None])).astype(BF16)

    params = pltpu.CompilerParams(dimension_semantics=("arbitrary", "arbitrary"),
                                  vmem_limit_bytes=VMEM_LIMIT_BYTES)
    tile_spec = lambda rows: pl.BlockSpec((None, rows, D_MODEL), lambda b, s: (b, s, 0))
    shared2 = lambda a: pl.BlockSpec(a.shape, lambda b, s: (0, 0), pipeline_mode=pl.Buffered(1))

    wkv_b = xattn_w_kv.astype(BF16)
    wo_b = rwkv_w_o.astype(BF16)
    cwo_b = conv_w_o.astype(BF16)
    xwo_b = xattn_w_o.astype(BF16)
    wout_b = w_out.astype(BF16)
    fup_b = ffn_w_up.astype(BF16)
    fdn_b = ffn_w_down.astype(BF16)
    fg = final_norm_g.reshape(1, D_MODEL)

    for layer in range(depth):
        cs = functools.partial(_const_spec, layer=layer)
        kt, vb = pl.pallas_call(
            _kv_kernel,
            grid=(batch,),
            in_specs=[pl.BlockSpec((None, N_MEM, D_MODEL), lambda b: (b, 0, 0)),
                      pl.BlockSpec((None, 1, D_MODEL), lambda b: (layer, 0, 0)),
                      pl.BlockSpec((None, D_MODEL, 2 * D_XATTN), lambda b: (layer, 0, 0))],
            out_specs=[pl.BlockSpec((None, D_XATTN, XATTN_HEADS * N_MEM), lambda b: (b, 0, 0)),
                       pl.BlockSpec((None, XATTN_HEADS * N_MEM, D_XATTN), lambda b: (b, 0, 0))],
            out_shape=[jax.ShapeDtypeStruct((batch, D_XATTN, XATTN_HEADS * N_MEM), BF16),
                       jax.ShapeDtypeStruct((batch, XATTN_HEADS * N_MEM, D_XATTN), BF16)],
            name=f"xattn_kv_l{layer}",
        )(mem, _row(mem_norm_g), wkv_b)

        mixer_in = [
            (x, tile_spec(ts)),
            (kt, pl.BlockSpec((None, D_XATTN, XATTN_HEADS * N_MEM), lambda b, s: (b, 0, 0))),
            (vb, pl.BlockSpec((None, XATTN_HEADS * N_MEM, D_XATTN), lambda b, s: (b, 0, 0))),
        ]
        for arr in (_row(attn_norm_g), win_p, _row(b_gate), mu_p, _row(rwkv_w0), wup_p,
                    _row(rwkv_a0), aup_p, gup_p, _row(rwkv_k_k), _row(rwkv_k_a),
                    _row(rwkv_r_k.reshape(depth, D_RWKV)), _row(rwkv_gn_w), _row(rwkv_gn_b),
                    wo_b, conv_w_dw, _row(conv_b_dw), _row(conv_ln_g), _row(conv_ln_b), cwo_b,
                    _row(conv_b_o), xwo_b, wout_b):
            mixer_in.append((arr, cs(arr.shape)))
        mixer_in += [(ones_bd, shared2(ones_bd)), (tri, shared2(tri))]
        x = pl.pallas_call(
            _mixer_kernel,
            grid=(batch, seq // ts),
            in_specs=[spec for _, spec in mixer_in],
            out_specs=tile_spec(ts),
            out_shape=jax.ShapeDtypeStruct(x.shape, F32),
            scratch_shapes=[pltpu.VMEM((SUBLANES, C_CONV), F32),
                            pltpu.VMEM((HIST, D_CONV), F32),
                            pltpu.VMEM((N_PAIR, PAIR, PAIR), F32),
                            pltpu.VMEM((HIST + ts, D_CONV), F32),
                            pltpu.VMEM((HIST + ts, D_CONV), F32)],
            compiler_params=params,
            name=f"mixer_l{layer}",
        )(*[arr for arr, _ in mixer_in])

        ffn_in = [(x, tile_spec(ts_ffn))]
        for arr in (_row(ffn_norm_g), fup_b, ffn_w_dw, _row(ffn_b_dw), fdn_b):
            ffn_in.append((arr, cs(arr.shape)))
        ffn_in.append((fg, shared2(fg)))
        x = pl.pallas_call(
            functools.partial(_ffn_kernel, final_norm=(layer == depth - 1)),
            grid=(batch, seq // ts_ffn),
            in_specs=[spec for _, spec in ffn_in],
            out_specs=tile_spec(ts_ffn),
            out_shape=jax.ShapeDtypeStruct(x.shape, F32),
            scratch_shapes=[pltpu.VMEM((SUBLANES, 2 * D_FF), F32)],
            compiler_params=params,
            name=f"ffn_l{layer}",
        )(*[arr for arr, _ in ffn_in])
    return x
```
